```python
import math
import jax, jax.numpy as jnp
from jax import lax
import numpy as np

D_MODEL = 1024
BATCH = 8
SEQ = 4096
DEPTH = 4
DEC_BATCH = 16
DEC_SEQ = 64
PAST_LEN = 2048

CHUNK = 64
N_META = 16
D_RNN = D_MODEL
N_RNN_BLOCKS = 8
RNN_BLOCK = D_RNN // N_RNN_BLOCKS
CONV_W = 4
LRU_C = 8.0
N_HEADS = 8
HEAD_DIM = D_MODEL // (2 * N_HEADS)
V_DIM = 2 * HEAD_DIM
QK_W = N_HEADS * 2 * HEAD_DIM
D_ATT = N_HEADS * V_DIM
IN_COLS = 2 * D_RNN + 2 * QK_W + 2 * D_ATT + 2 * D_MODEL
Q_BLOCK = 128
ROPE_THETA = 10000.0
EPS = 1e-6

kernel_name = "hawk_diffattn_gated_stream_step"

F32 = jnp.float32


def rms_norm(x, w):
    xf = x.astype(F32)
    y = xf * lax.rsqrt(jnp.mean(xf * xf, axis=-1, keepdims=True) + EPS)
    return (y * w.astype(F32)).astype(x.dtype)


def rope(x, pos):
    half = HEAD_DIM // 2
    inv = 1.0 / (ROPE_THETA ** (jnp.arange(half, dtype=F32) / half))
    ang = pos.astype(F32)[:, None] * inv[None, :]
    cos = jnp.cos(ang)[None, :, None, None, :]
    sin = jnp.sin(ang)[None, :, None, None, :]
    xf = x.astype(F32)
    x1, x2 = xf[..., :half], xf[..., half:]
    return jnp.concatenate([x1 * cos - x2 * sin, x2 * cos + x1 * sin], axis=-1).astype(x.dtype)


def in_projection(h, norm_w, w_in):
    z = rms_norm(h, norm_w) @ w_in
    parts = []
    off = 0
    for width in (D_RNN, D_RNN, QK_W, QK_W, D_ATT, D_ATT, D_MODEL, D_MODEL):
        parts.append(z[..., off:off + width])
        off += width
    return parts


def causal_conv(x, prefix, conv_w, conv_b):
    T = x.shape[1]
    xp = jnp.concatenate([prefix.astype(x.dtype), x], axis=1)
    y = conv_b + xp[:, 0:T] * conv_w[0]
    for k in range(1, CONV_W):
        y = y + xp[:, k:k + T] * conv_w[k]
    return y, xp[:, -(CONV_W - 1):]


def block_diag(x, w, b):
    B, T, _ = x.shape
    xb = x.reshape(B, T, N_RNN_BLOCKS, RNN_BLOCK)
    return jnp.einsum('btni,nij->btnj', xb, w).reshape(B, T, D_RNN) + b


def rglru_branch(xr, zr, prefix, h0, conv_w, conv_b, w_rg, b_rg, w_ig, b_ig, lru_lambda):
    xc, new_prefix = causal_conv(xr, prefix, conv_w, conv_b)
    r = jax.nn.sigmoid(block_diag(xc, w_rg, b_rg).astype(F32))
    i = jax.nn.sigmoid(block_diag(xc, w_ig, b_ig).astype(F32))
    log_a = -LRU_C * r * jax.nn.softplus(-lru_lambda.astype(F32))
    a = jnp.exp(log_a)
    mult = jnp.sqrt(-jnp.expm1(2.0 * log_a))
    b = mult * i * xc.astype(F32)
    b = b.at[:, 0].add(a[:, 0] * h0.astype(F32))

    def combine(lhs, rhs):
        a1, b1 = lhs
        a2, b2 = rhs
        return a1 * a2, a2 * b1 + b2

    _, hs = lax.associative_scan(combine, (a, b), axis=1)
    out = hs.astype(xr.dtype) * jax.nn.silu(zr)
    return out, new_prefix, hs[:, -1].astype(xr.dtype)


def diff_core(q, k, v, lam, mask):
    s = jnp.einsum('bqhcd,bkhcd->bhcqk', q, k).astype(F32) * (HEAD_DIM ** -0.5)
    if mask is not None:
        s = jnp.where(mask, s, -jnp.inf)
    p = jax.nn.softmax(s, axis=-1)
    attn = p[:, :, 0] - lam * p[:, :, 1]
    return jnp.einsum('bhqk,bkhe->bqhe', attn.astype(v.dtype), v)


def prompt_attention(q, k, v, lam):
    B, T = q.shape[:2]
    S = T - N_META
    pos = jnp.arange(T)
    chunk_id = jnp.where(pos < N_META, 0, (pos - N_META) // CHUNK + 1)
    o_meta = diff_core(q[:, :N_META], k[:, :N_META], v[:, :N_META], lam, None)
    nb = S // Q_BLOCK
    qf = q[:, N_META:].reshape(B, nb, Q_BLOCK, N_HEADS, 2, HEAD_DIM).swapaxes(0, 1)
    qc = chunk_id[N_META:].reshape(nb, Q_BLOCK)

    def one_block(args):
        qb, cb = args
        mask = chunk_id[None, :] <= cb[:, None]
        return diff_core(qb, k, v, lam, mask)

    o_f = lax.map(one_block, (qf, qc))
    o_f = o_f.swapaxes(0, 1).reshape(B, S, N_HEADS, V_DIM)
    return jnp.concatenate([o_meta, o_f], axis=1)


def diff_attn_finish(o, za, subln_w, lam_init):
    o = rms_norm(o, subln_w) * (1.0 - lam_init)
    B, T = o.shape[:2]
    return o.reshape(B, T, D_ATT) * jax.nn.silu(za)


def merge_out(o_rnn, o_att, g_rnn, g_att, w_proj_rnn, w_proj_att, w_out):
    m = jax.nn.sigmoid(g_rnn) * (o_rnn @ w_proj_rnn) + jax.nn.sigmoid(g_att) * (o_att @ w_proj_att)
    return m @ w_out


def setup_inputs(seed: int = 0) -> dict:
    key = jax.random.key(seed)
    ks = jax.random.split(key, 32)

    def nrm(k, shape, scale):
        return jax.random.normal(k, shape, F32) * scale

    u = jax.random.uniform(ks[12], (DEPTH, D_RNN), F32, 0.9, 0.999)
    return {
        "x_prompt": nrm(ks[0], (BATCH, SEQ, D_MODEL), 1.0),
        "x_sample": nrm(ks[1], (DEC_BATCH, DEC_SEQ, D_MODEL), 1.0),
        "cache_k": nrm(ks[2], (DEPTH, DEC_BATCH, PAST_LEN, N_HEADS, 2, HEAD_DIM), 1.0),
        "cache_v": nrm(ks[3], (DEPTH, DEC_BATCH, PAST_LEN, N_HEADS, V_DIM), 1.0),
        "state_conv": nrm(ks[4], (DEPTH, DEC_BATCH, CONV_W - 1, D_RNN), 1.0),
        "state_rnn": nrm(ks[5], (DEPTH, DEC_BATCH, D_RNN), 0.5),
        "meta_tokens": nrm(ks[6], (N_META, D_MODEL), 1.0),
        "norm_w": 1.0 + nrm(ks[7], (DEPTH, D_MODEL), 0.01),
        "w_in": nrm(ks[8], (DEPTH, D_MODEL, IN_COLS), D_MODEL ** -0.5),
        "conv_w": nrm(ks[9], (DEPTH, CONV_W, D_RNN), CONV_W ** -0.5),
        "conv_b": nrm(ks[10], (DEPTH, D_RNN), 0.01),
        "w_rg": nrm(ks[11], (DEPTH, N_RNN_BLOCKS, RNN_BLOCK, RNN_BLOCK), RNN_BLOCK ** -0.5),
        "b_rg": nrm(ks[13], (DEPTH, D_RNN), 0.01),
        "w_ig": nrm(ks[14], (DEPTH, N_RNN_BLOCKS, RNN_BLOCK, RNN_BLOCK), RNN_BLOCK ** -0.5),
        "b_ig": nrm(ks[15], (DEPTH, D_RNN), 0.01),
        "lru_lambda": jnp.log(u) - jnp.log1p(-u),
        "lambda_q1": nrm(ks[16], (DEPTH, HEAD_DIM), 0.1),
        "lambda_k1": nrm(ks[17], (DEPTH, HEAD_DIM), 0.1),
        "lambda_q2": nrm(ks[18], (DEPTH, HEAD_DIM), 0.1),
        "lambda_k2": nrm(ks[19], (DEPTH, HEAD_DIM), 0.1),
        "subln_w": 1.0 + nrm(ks[20], (DEPTH, V_DIM), 0.01),
        "w_proj_rnn": nrm(ks[21], (DEPTH, D_RNN, D_MODEL), D_RNN ** -0.5),
        "w_proj_att": nrm(ks[22], (DEPTH, D_ATT, D_MODEL), D_ATT ** -0.5),
        "w_out": nrm(ks[23], (DEPTH, D_MODEL, D_MODEL), D_MODEL ** -0.5),
        "final_norm_w": 1.0 + nrm(ks[24], (D_MODEL,), 0.01),
    }


def reference(x_prompt, x_sample, cache_k, cache_v, state_conv, state_rnn, meta_tokens,
              norm_w, w_in, conv_w, conv_b, w_rg, b_rg, w_ig, b_ig, lru_lambda,
              lambda_q1, lambda_k1, lambda_q2, lambda_k2, subln_w,
              w_proj_rnn, w_proj_att, w_out, final_norm_w):
    B = x_prompt.shape[0]
    DB, S = x_sample.shape[0], x_sample.shape[1]
    dt = x_prompt.dtype
    hp = jnp.concatenate(
        [jnp.broadcast_to(meta_tokens[None].astype(dt), (B, N_META, D_MODEL)), x_prompt], axis=1)
    hs = x_sample
    Tp = hp.shape[1]
    past = cache_k.shape[2]
    pos_p = jnp.arange(Tp)
    pos_s = N_META + past + jnp.arange(S)

    kp_l, vp_l, cp_l, rp_l = [], [], [], []
    ks_l, vs_l, cs_l, rs_l = [], [], [], []
    for l in range(DEPTH):
        lam_init = 0.8 - 0.6 * math.exp(-0.3 * l)
        lam = (jnp.exp(jnp.sum(lambda_q1[l].astype(F32) * lambda_k1[l].astype(F32)))
               - jnp.exp(jnp.sum(lambda_q2[l].astype(F32) * lambda_k2[l].astype(F32))) + lam_init)
        rnn_args = (conv_w[l], conv_b[l], w_rg[l], b_rg[l], w_ig[l], b_ig[l], lru_lambda[l])

        xr, zr, q, k, v, za, g_r, g_a = in_projection(hp, norm_w[l], w_in[l])
        q = rope(q.reshape(B, Tp, N_HEADS, 2, HEAD_DIM), pos_p)
        k = rope(k.reshape(B, Tp, N_HEADS, 2, HEAD_DIM), pos_p)
        v = v.reshape(B, Tp, N_HEADS, V_DIM)
        o_r, cp, rp = rglru_branch(xr, zr, jnp.zeros((B, CONV_W - 1, D_RNN), dt),
                                   jnp.zeros((B, D_RNN), dt), *rnn_args)
        o_a = diff_attn_finish(prompt_attention(q, k, v, lam), za, subln_w[l], lam_init)
        hp = hp + merge_out(o_r, o_a, g_r, g_a, w_proj_rnn[l], w_proj_att[l], w_out[l])
        kp_l.append(k)
        vp_l.append(v)
        cp_l.append(cp)
        rp_l.append(rp)

        xr, zr, q, k, v, za, g_r, g_a = in_projection(hs, norm_w[l], w_in[l])
        q = rope(q.reshape(DB, S, N_HEADS, 2, HEAD_DIM), pos_s)
        k = rope(k.reshape(DB, S, N_HEADS, 2, HEAD_DIM), pos_s)
        v = v.reshape(DB, S, N_HEADS, V_DIM)
        o_r, cs, rs = rglru_branch(xr, zr, state_conv[l], state_rnn[l], *rnn_args)
        k_all = jnp.concatenate([cache_k[l].astype(k.dtype), k], axis=1)
        v_all = jnp.concatenate([cache_v[l].astype(v.dtype), v], axis=1)
        o_a = diff_attn_finish(diff_core(q, k_all, v_all, lam, None), za, subln_w[l], lam_init)
        hs = hs + merge_out(o_r, o_a, g_r, g_a, w_proj_rnn[l], w_proj_att[l], w_out[l])
        ks_l.append(k)
        vs_l.append(v)
        cs_l.append(cs)
        rs_l.append(rs)

    y_prompt = rms_norm(hp[:, N_META:], final_norm_w)
    y_sample = rms_norm(hs, final_norm_w)
    return (y_prompt, y_sample,
            jnp.stack(kp_l), jnp.stack(vp_l), jnp.stack(cp_l), jnp.stack(rp_l),
            jnp.stack(ks_l), jnp.stack(vs_l), jnp.stack(cs_l), jnp.stack(rs_l))
```

```python
import functools
import math

import jax
import jax.numpy as jnp
from jax import lax
from jax.experimental import pallas as pl
from jax.experimental.pallas import tpu as pltpu

F32 = jnp.float32
BF16 = jnp.bfloat16

D_MODEL = 1024
N_HEADS = 8
HEAD_DIM = 64
V_DIM = 128
CHUNK = 64
CHUNK_SHIFT = 6
N_META = 16
CONV_W = 4
LRU_C = 8.0
N_RNN_BLOCKS = 8
RNN_BLOCK = 128
ROPE_THETA = 10000.0
EPS = 1e-6
N_GROUPS = 8
LANES = 128
SUBLANES = 8
G_XR, G_ZR, G_Q, G_K, G_V, G_ZA, G_GR, G_GA = range(8)
Q_SCALE = (HEAD_DIM ** -0.5) * math.log2(math.e)
NEG_BIG = -1e30
VMEM_LIMIT = 56 * 1024 * 1024


def _sds(shape, dtype):
    return jax.ShapeDtypeStruct(shape, dtype)


def _nt_dot(a, b):
    return lax.dot_general(a, b, (((1,), (1,)), ((), ())), preferred_element_type=F32)


def _inproj_body(x_ref, nw_ref, w_ref, cos_ref, sin_ref, z_ref, kv_ref, xn_ref, *, tm):
    j = pl.program_id(1)

    @pl.when(j == 0)
    def _():
        x = x_ref[...]
        ms = jnp.mean(x * x, axis=-1, keepdims=True)
        xn_ref[...] = (x * lax.rsqrt(ms + EPS) * nw_ref[...]).astype(BF16)

    acc = jnp.dot(xn_ref[...], w_ref[...], preferred_element_type=F32)

    def rope_slab(s, first_half):
        rot = jnp.where(first_half, pltpu.roll(s, LANES - 32, 1), pltpu.roll(s, 32, 1))
        return s * cos_ref[...] + rot * sin_ref[...]

    @pl.when(j == G_XR)
    def _():
        z_ref[...] = acc.astype(BF16)

    @pl.when((j == G_ZR) | (j == G_ZA))
    def _():
        z_ref[...] = (acc * jax.nn.sigmoid(acc)).astype(BF16)

    @pl.when(j == G_Q)
    def _():
        first_half = (lax.broadcasted_iota(jnp.int32, (tm, LANES), 1) & 63) < 32
        for hh in range(N_HEADS):
            sl = slice(hh * LANES, (hh + 1) * LANES)
            z_ref[:, sl] = (rope_slab(acc[:, sl], first_half) * Q_SCALE).astype(BF16)

    @pl.when(j == G_K)
    def _():
        first_half = (lax.broadcasted_iota(jnp.int32, (tm, LANES), 1) & 63) < 32
        for hh in range(N_HEADS):
            sl = slice(hh * LANES, (hh + 1) * LANES)
            r = rope_slab(acc[:, sl], first_half)
            kv_ref[:, sl] = r
            z_ref[:, sl] = r.astype(BF16)

    @pl.when(j == G_V)
    def _():
        kv_ref[...] = acc
        z_ref[...] = acc.astype(BF16)

    @pl.when(j >= G_GR)
    def _():
        z_ref[...] = jax.nn.sigmoid(acc).astype(BF16)


def _inproj(h2d, nw, w_bf, cos, sin, *, tm):
    n = h2d.shape[0]
    ntab = cos.shape[0] // tm

    def kv_index(i, j):
        return (i, jnp.where(j > G_K, 1, 0))

    return pl.pallas_call(
        functools.partial(_inproj_body, tm=tm),
        grid=(n // tm, N_GROUPS),
        in_specs=[
            pl.BlockSpec((tm, D_MODEL), lambda i, j: (i, 0)),
            pl.BlockSpec((1, D_MODEL), lambda i, j: (0, 0)),
            pl.BlockSpec((D_MODEL, D_MODEL), lambda i, j: (0, j)),
            pl.BlockSpec((tm, LANES), lambda i, j: (i % ntab, 0)),
            pl.BlockSpec((tm, LANES), lambda i, j: (i % ntab, 0)),
        ],
        out_specs=[
            pl.BlockSpec((tm, D_MODEL), lambda i, j: (i, j)),
            pl.BlockSpec((tm, D_MODEL), kv_index),
        ],
        out_shape=[_sds((n, N_GROUPS * D_MODEL), BF16), _sds((n, 2 * D_MODEL), F32)],
        scratch_shapes=[pltpu.VMEM((tm, D_MODEL), BF16)],
        compiler_params=pltpu.CompilerParams(
            dimension_semantics=("arbitrary", "arbitrary"), vmem_limit_bytes=VMEM_LIMIT),
        name="inproj",
    )(h2d, nw, w_bf, cos, sin)


def _rglru_body(xr_ref, sz_ref, pre_ref, h0_ref, cw_ref, cb_ref, wg_ref, brg_ref, big_ref, lam_ref,
                o_ref, cst_ref, hst_ref, xbuf, a_s, b_s, h_s, hc, *, tt):
    t = pl.program_id(1)

    @pl.when(t == 0)
    def _():
        xbuf[0:SUBLANES, :] = pre_ref[0]
        hc[...] = jnp.broadcast_to(h0_ref[0], (SUBLANES, D_MODEL))

    xr = xr_ref[...].astype(F32)
    xbuf[SUBLANES:SUBLANES + tt, :] = xr
    cw = cw_ref[...]
    xc = cb_ref[...] + xbuf[SUBLANES - 3:SUBLANES - 3 + tt, :] * cw[0:1, :]
    xc = xc + xbuf[SUBLANES - 2:SUBLANES - 2 + tt, :] * cw[1:2, :]
    xc = xc + xbuf[SUBLANES - 1:SUBLANES - 1 + tt, :] * cw[2:3, :]
    xc = xc + xr * cw[3:4, :]
    xcb = xc.astype(BF16)

    nl = -lam_ref[...]
    softplus = jnp.maximum(nl, 0.0) + jnp.log(1.0 + jnp.exp(-jnp.abs(nl)))
    c = -LRU_C * softplus

    rowmod = lax.broadcasted_iota(jnp.int32, (tt, RNN_BLOCK), 0) & (SUBLANES - 1)
    for n in range(N_RNN_BLOCKS):
        sl = slice(n * RNN_BLOCK, (n + 1) * RNN_BLOCK)
        g = jnp.dot(xcb[:, sl], wg_ref[n], preferred_element_type=F32)
        r = jax.nn.sigmoid(g[:, :RNN_BLOCK] + brg_ref[:, sl])
        ig = jax.nn.sigmoid(g[:, RNN_BLOCK:] + big_ref[:, sl])
        log_a = c[:, sl] * r
        a = jnp.exp(log_a)
        mult = jnp.sqrt(-jnp.tanh(log_a) * (1.0 + a * a))
        bb = mult * ig * xc[:, sl]
        for d in (1, 2, 4):
            keep = rowmod >= d
            a_p = jnp.where(keep, pltpu.roll(a, d, 0), 1.0)
            b_p = jnp.where(keep, pltpu.roll(bb, d, 0), 0.0)
            bb = bb + a * b_p
            a = a * a_p
        a_s[:, sl] = a
        b_s[:, sl] = bb

    def group(gi, h):
        rows = pl.ds(pl.multiple_of(gi * SUBLANES, SUBLANES), SUBLANES)
        hb = a_s[rows, :] * h + b_s[rows, :]
        h_s[rows, :] = hb
        return jnp.broadcast_to(hb[SUBLANES - 1:SUBLANES, :], (SUBLANES, D_MODEL))

    h = lax.fori_loop(0, tt // SUBLANES, group, hc[...])
    hc[...] = h
    o_ref[...] = (h_s[...] * sz_ref[...].astype(F32)).astype(BF16)
    hst_ref[0] = h[0:1, :]
    tail = xbuf[tt:tt + SUBLANES, :]
    cst_ref[0] = tail
    xbuf[0:SUBLANES, :] = tail


def _rglru(z, prefix, h0, cw, cb, wg, brg, big, lam, *, bt, t_len, tt):
    nt = t_len // tt
    n = bt * t_len
    vec = lambda: pl.BlockSpec((1, D_MODEL), lambda b, t: (0, 0))
    return pl.pallas_call(
        functools.partial(_rglru_body, tt=tt),
        grid=(bt, nt),
        in_specs=[
            pl.BlockSpec((tt, D_MODEL), lambda b, t: (b * nt + t, G_XR)),
            pl.BlockSpec((tt, D_MODEL), lambda b, t: (b * nt + t, G_ZR)),
            pl.BlockSpec((1, SUBLANES, D_MODEL), lambda b, t: (b, 0, 0)),
            pl.BlockSpec((1, 1, D_MODEL), lambda b, t: (b, 0, 0)),
            pl.BlockSpec((CONV_W, D_MODEL), lambda b, t: (0, 0)),
            vec(),
            pl.BlockSpec((N_RNN_BLOCKS, RNN_BLOCK, 2 * RNN_BLOCK), lambda b, t: (0, 0, 0)),
            vec(), vec(), vec(),
        ],
        out_specs=[
            pl.BlockSpec((tt, D_MODEL), lambda b, t: (b * nt + t, 0)),
            pl.BlockSpec((1, SUBLANES, D_MODEL), lambda b, t: (b, 0, 0)),
            pl.BlockSpec((1, 1, D_MODEL), lambda b, t: (b, 0, 0)),
        ],
        out_shape=[_sds((n, D_MODEL), BF16), _sds((bt, SUBLANES, D_MODEL), F32),
                   _sds((bt, 1, D_MODEL), F32)],
        scratch_shapes=[
            pltpu.VMEM((tt + SUBLANES, D_MODEL), F32),
            pltpu.VMEM((tt, D_MODEL), F32),
            pltpu.VMEM((tt, D_MODEL), F32),
            pltpu.VMEM((tt, D_MODEL), F32),
            pltpu.VMEM((SUBLANES, D_MODEL), F32),
        ],
        compiler_params=pltpu.CompilerParams(
            dimension_semantics=("arbitrary", "arbitrary"), vmem_limit_bytes=VMEM_LIMIT),
        name="rglru",
    )(z, z, prefix, h0, cw, cb, wg, brg, big, lam)


def _stack_maps(q):
    lane = lax.broadcasted_iota(jnp.int32, q.shape, 1)
    zero = jnp.zeros_like(q)
    return jnp.concatenate(
        [jnp.where(lane < HEAD_DIM, q, zero), jnp.where(lane >= HEAD_DIM, q, zero)], axis=0)


def _softmax_update(pairs, m_s, l_s, acc_s):
    m_prev = m_s[...]
    m_new = m_prev
    for s, _ in pairs:
        m_new = jnp.maximum(m_new, jnp.max(s, axis=-1, keepdims=True))
    alpha = jnp.exp2(m_prev - m_new)
    l_new = alpha * l_s[...]
    acc = alpha * acc_s[...]
    for s, v in pairs:
        p = jnp.exp2(s - m_new)
        l_new = l_new + jnp.sum(p, axis=-1, keepdims=True)
        acc = acc + jnp.dot(p.astype(BF16), v, preferred_element_type=F32)
    m_s[...] = m_new
    l_s[...] = l_new
    acc_s[...] = acc


def _attn_finish(acc, l, sza, subw, lq1, lk1, lq2, lk2, tq, lam_init):
    o = acc * (1.0 / l)
    lam = (jnp.exp(jnp.sum(lq1 * lk1, axis=-1, keepdims=True))
           - jnp.exp(jnp.sum(lq2 * lk2, axis=-1, keepdims=True)) + lam_init)
    od = o[:tq, :] - lam * o[tq:, :]
    ms = jnp.mean(od * od, axis=-1, keepdims=True)
    on = (od * lax.rsqrt(ms + EPS) * subw) * (1.0 - lam_init)
    return (on * sza.astype(F32)).astype(BF16)


def _attn_frames_body(q_ref, k_ref, v_ref, km_ref, vm_ref, sza_ref, sub_ref, lq1_ref, lk1_ref,
                      lq2_ref, lk2_ref, o_ref, m_s, l_s, acc_s, *, tq, tkb, n_extra, lam_init):
    i = pl.program_id(2)
    q2 = _stack_maps(q_ref[...])
    m_s[...] = jnp.full(m_s.shape, NEG_BIG, F32)
    l_s[...] = jnp.zeros(l_s.shape, F32)
    acc_s[...] = jnp.zeros(acc_s.shape, F32)

    nfull = (i * tq) // tkb

    def step(c, carry):
        off = pl.multiple_of(c * tkb, tkb)
        s = _nt_dot(q2, k_ref[pl.ds(off, tkb), :])
        _softmax_update([(s, v_ref[pl.ds(off, tkb), :])], m_s, l_s, acc_s)
        return carry

    lax.fori_loop(0, nfull, step, 0)

    off = pl.multiple_of(nfull * tkb, tkb)
    s = _nt_dot(q2, k_ref[pl.ds(off, tkb), :])
    row = lax.broadcasted_iota(jnp.int32, (2 * tq, tkb), 0)
    col = lax.broadcasted_iota(jnp.int32, (2 * tq, tkb), 1)
    q_chunk = (i * tq + (row & (tq - 1))) >> CHUNK_SHIFT
    k_chunk = (off + col) >> CHUNK_SHIFT
    s = jnp.where(k_chunk <= q_chunk, s, -jnp.inf)
    sm = _nt_dot(q2, km_ref[...])
    colm = lax.broadcasted_iota(jnp.int32, sm.shape, 1)
    sm = jnp.where(colm < n_extra, sm, -jnp.inf)
    _softmax_update([(s, v_ref[pl.ds(off, tkb), :]), (sm, vm_ref[...])], m_s, l_s, acc_s)

    o_ref[...] = _attn_finish(acc_s[...], l_s[...], sza_ref[...], sub_ref[...], lq1_ref[...],
                              lk1_ref[...], lq2_ref[...], lk2_ref[...], tq, lam_init)


def _attn_frames(z, km, vm, subw, lq1, lk1, lq2, lk2, *, bt, t_len, tq, tkb, n_extra, lam_init):
    nq = t_len // tq
    n = bt * t_len
    hcol = lambda g: (g * D_MODEL) // LANES
    small = lambda w: pl.BlockSpec((1, w), lambda b, h, i: (0, 0))
    return pl.pallas_call(
        functools.partial(_attn_frames_body, tq=tq, tkb=tkb, n_extra=n_extra, lam_init=lam_init),
        grid=(bt, N_HEADS, nq),
        in_specs=[
            pl.BlockSpec((tq, LANES), lambda b, h, i: (b * nq + i, hcol(G_Q) + h)),
            pl.BlockSpec((t_len, LANES), lambda b, h, i: (b, hcol(G_K) + h)),
            pl.BlockSpec((t_len, LANES), lambda b, h, i: (b, hcol(G_V) + h)),
            pl.BlockSpec((LANES, LANES), lambda b, h, i: (0, h)),
            pl.BlockSpec((LANES, LANES), lambda b, h, i: (0, h)),
            pl.BlockSpec((tq, LANES), lambda b, h, i: (b * nq + i, hcol(G_ZA) + h)),
            small(V_DIM), small(HEAD_DIM), small(HEAD_DIM), small(HEAD_DIM), small(HEAD_DIM),
        ],
        out_specs=pl.BlockSpec((tq, LANES), lambda b, h, i: (b * nq + i, h)),
        out_shape=_sds((n, D_MODEL), BF16),
        scratch_shapes=[
            pltpu.VMEM((2 * tq, 1), F32),
            pltpu.VMEM((2 * tq, 1), F32),
            pltpu.VMEM((2 * tq, V_DIM), F32),
        ],
        compiler_params=pltpu.CompilerParams(
            dimension_semantics=("arbitrary", "arbitrary", "arbitrary"),
            vmem_limit_bytes=VMEM_LIMIT),
        name="attn_frames",
    )(z, z, z, km, vm, z, subw, lq1, lk1, lq2, lk2)


def _attn_small_body(*refs, tq, n_own, has_cache, lam_init):
    if has_cache:
        (q_ref, kc_ref, vc_ref, ko_ref, vo_ref, sza_ref, sub_ref, lq1_ref, lk1_ref, lq2_ref,
         lk2_ref, o_ref, m_s, l_s, acc_s) = refs
    else:
        (q_ref, ko_ref, vo_ref, sza_ref, sub_ref, lq1_ref, lk1_ref, lq2_ref,
         lk2_ref, o_ref, m_s, l_s, acc_s) = refs
    q2 = _stack_maps(q_ref[...])
    m_s[...] = jnp.full(m_s.shape, NEG_BIG, F32)
    l_s[...] = jnp.zeros(l_s.shape, F32)
    acc_s[...] = jnp.zeros(acc_s.shape, F32)
    pairs = []
    if has_cache:
        pairs.append((_nt_dot(q2, kc_ref[0, 0].astype(BF16)), vc_ref[0, 0].astype(BF16)))
    so = _nt_dot(q2, ko_ref[...])
    colo = lax.broadcasted_iota(jnp.int32, so.shape, 1)
    so = jnp.where(colo < n_own, so, -jnp.inf)
    pairs.append((so, vo_ref[...]))
    _softmax_update(pairs, m_s, l_s, acc_s)
    o_ref[...] = _attn_finish(acc_s[...], l_s[...], sza_ref[...], sub_ref[...], lq1_ref[...],
                              lk1_ref[...], lq2_ref[...], lk2_ref[...], tq, lam_init)


def _attn_small(z, ko, vo, cache, subw, lq1, lk1, lq2, lk2, *, bt, tq, n_own, layer, lam_init):
    hcol = lambda g: (g * D_MODEL) // LANES
    small = lambda w: pl.BlockSpec((1, w), lambda b, h: (0, 0))
    has_cache = cache is not None
    in_specs = [pl.BlockSpec((tq, LANES), lambda b, h: (b, hcol(G_Q) + h))]
    args = [z]
    if has_cache:
        past = cache[0].shape[2]
        in_specs += [pl.BlockSpec((1, 1, past, LANES), lambda b, h: (layer, b, 0, h))] * 2
        args += list(cache)
    in_specs += [
        pl.BlockSpec((LANES, LANES), lambda b, h: (b, h)),
        pl.BlockSpec((LANES, LANES), lambda b, h: (b, h)),
        pl.BlockSpec((tq, LANES), lambda b, h: (b, hcol(G_ZA) + h)),
        small(V_DIM), small(HEAD_DIM), small(HEAD_DIM), small(HEAD_DIM), small(HEAD_DIM),
    ]
    args += [ko, vo, z, subw, lq1, lk1, lq2, lk2]
    return pl.pallas_call(
        functools.partial(_attn_small_body, tq=tq, n_own=n_own, has_cache=has_cache,
                          lam_init=lam_init),
        grid=(bt, N_HEADS),
        in_specs=in_specs,
        out_specs=pl.BlockSpec((tq, LANES), lambda b, h: (b, h)),
        out_shape=_sds((bt * tq, D_MODEL), BF16),
        scratch_shapes=[
            pltpu.VMEM((2 * tq, 1), F32),
            pltpu.VMEM((2 * tq, 1), F32),
            pltpu.VMEM((2 * tq, V_DIM), F32),
        ],
        compiler_params=pltpu.CompilerParams(
            dimension_semantics=("arbitrary", "arbitrary"), vmem_limit_bytes=VMEM_LIMIT),
        name="attn_small",
    )(*args)


def _merge_body(h_ref, or_ref, oa_ref, gr_ref, ga_ref, wpr_ref, wpa_ref, wo_ref, fnw_ref, out_ref,
                *, final):
    pr = jnp.dot(or_ref[...], wpr_ref[...], preferred_element_type=F32)
    pa = jnp.dot(oa_ref[...], wpa_ref[...], preferred_element_type=F32)
    m = gr_ref[...].astype(F32) * pr + ga_ref[...].astype(F32) * pa
    hn = h_ref[...] + jnp.dot(m.astype(BF16), wo_ref[...], preferred_element_type=F32)
    if final:
        ms = jnp.mean(hn * hn, axis=-1, keepdims=True)
        hn = hn * lax.rsqrt(ms + EPS) * fnw_ref[...]
    out_ref[...] = hn


def _merge(h2d, o_r, o_a, z, wpr, wpa, wo, fnw, *, tm, final):
    n = h2d.shape[0]
    tile = lambda: pl.BlockSpec((tm, D_MODEL), lambda i: (i, 0))
    wspec = lambda: pl.BlockSpec((D_MODEL, D_MODEL), lambda i: (0, 0))
    return pl.pallas_call(
        functools.partial(_merge_body, final=final),
        grid=(n // tm,),
        in_specs=[
            tile(), tile(), tile(),
            pl.BlockSpec((tm, D_MODEL), lambda i: (i, G_GR)),
            pl.BlockSpec((tm, D_MODEL), lambda i: (i, G_GA)),
            wspec(), wspec(), wspec(),
            pl.BlockSpec((1, D_MODEL), lambda i: (0, 0)),
        ],
        out_specs=tile(),
        out_shape=_sds((n, D_MODEL), F32),
        compiler_params=pltpu.CompilerParams(
            dimension_semantics=("arbitrary",), vmem_limit_bytes=VMEM_LIMIT),
        name="merge_out",
    )(h2d, o_r, o_a, z, z, wpr, wpa, wo, fnw)


def _rope_tables(pos):
    half = HEAD_DIM // 2
    inv = 1.0 / (ROPE_THETA ** (jnp.arange(half, dtype=F32) / half))
    ang = pos.astype(F32)[:, None] * inv[None, :]
    cos, sin = jnp.cos(ang), jnp.sin(ang)
    cos_t = jnp.concatenate([cos, cos, cos, cos], axis=1)
    sin_t = jnp.concatenate([-sin, sin, -sin, sin], axis=1)
    return cos_t, sin_t


def _pick_tile(n, pref):
    t = pref
    while n % t:
        t //= 2
    return t


def _pad_rows(x, rows):
    return jnp.pad(x, ((0, rows - x.shape[0]), (0, 0)))


def kernel(x_prompt, x_sample, cache_k, cache_v, state_conv, state_rnn, meta_tokens, norm_w, w_in,
           conv_w, conv_b, w_rg, b_rg, w_ig, b_ig, lru_lambda, lambda_q1, lambda_k1, lambda_q2,
           lambda_k2, subln_w, w_proj_rnn, w_proj_att, w_out, final_norm_w):
    B, SEQ, _ = x_prompt.shape
    DB, S, _ = x_sample.shape
    depth, _, past = cache_k.shape[0], cache_k.shape[1], cache_k.shape[2]
    assert SEQ % 1024 == 0 and S % 16 == 0 and S <= LANES

    w_in_bf = w_in.astype(BF16)
    wg_bf = jnp.concatenate([w_rg, w_ig], axis=-1).astype(BF16)
    wpr_bf, wpa_bf, wo_bf = w_proj_rnn.astype(BF16), w_proj_att.astype(BF16), w_out.astype(BF16)
    cache_k4 = cache_k.reshape(depth, DB, past, D_MODEL)
    cache_v4 = cache_v.reshape(depth, DB, past, D_MODEL)
    fnw = final_norm_w.reshape(1, D_MODEL)

    cos_m, sin_m = _rope_tables(jnp.arange(N_META))
    cos_f, sin_f = _rope_tables(N_META + jnp.arange(SEQ))
    cos_s, sin_s = _rope_tables(N_META + past + jnp.arange(S))
    cos_s, sin_s = jnp.tile(cos_s, (DB, 1)), jnp.tile(sin_s, (DB, 1))

    tm_f = _pick_tile(B * SEQ, 512)
    tm_s = _pick_tile(DB * S, 512)
    tt_f = _pick_tile(SEQ, 512)

    def layer_params(l):
        row = lambda a: a[l].reshape(1, -1)
        return dict(
            nw=row(norm_w), w_in=w_in_bf[l], cw=conv_w[l], cb=row(conv_b), wg=wg_bf[l],
            brg=row(b_rg), big=row(b_ig), lam=row(lru_lambda), subw=row(subln_w),
            lq1=row(lambda_q1), lk1=row(lambda_k1), lq2=row(lambda_q2), lk2=row(lambda_k2),
            wpr=wpr_bf[l], wpa=wpa_bf[l], wo=wo_bf[l],
            lam_init=0.8 - 0.6 * math.exp(-0.3 * l))

    def kcols(z):
        return z[:, G_K * D_MODEL:(G_K + 1) * D_MODEL]

    def vcols(z):
        return z[:, G_V * D_MODEL:(G_V + 1) * D_MODEL]

    hm = meta_tokens.astype(F32)
    hf = x_prompt.reshape(B * SEQ, D_MODEL)
    hs = x_sample.reshape(DB * S, D_MODEL)
    zero_prefix = jnp.zeros((1, SUBLANES, D_MODEL), F32)
    zero_h0 = jnp.zeros((1, 1, D_MODEL), F32)

    kp, vp, cp, rp, ks, vs, cs, rs = [], [], [], [], [], [], [], []
    for l in range(depth):
        p = layer_params(l)
        final = l == depth - 1
        lam_kw = dict(subw=p["subw"], lq1=p["lq1"], lk1=p["lk1"], lq2=p["lq2"], lk2=p["lk2"])
        rnn_w = (p["cw"], p["cb"], p["wg"], p["brg"], p["big"], p["lam"])

        zm, kvm = _inproj(hm, p["nw"], p["w_in"], cos_m, sin_m, tm=N_META)
        orm, cst_m, hst_m = _rglru(zm, zero_prefix, zero_h0, *rnn_w, bt=1, t_len=N_META, tt=N_META)
        km_pad, vm_pad = _pad_rows(kcols(zm), LANES), _pad_rows(vcols(zm), LANES)
        oam = _attn_small(zm, km_pad, vm_pad, None, **lam_kw, bt=1, tq=N_META, n_own=N_META,
                          layer=l, lam_init=p["lam_init"])
        hm = _merge(hm, orm, oam, zm, p["wpr"], p["wpa"], p["wo"], fnw, tm=N_META, final=False)

        zf, kvf = _inproj(hf, p["nw"], p["w_in"], cos_f, sin_f, tm=tm_f)
        orf, cst_f, hst_f = _rglru(
            zf, jnp.broadcast_to(cst_m, (B, SUBLANES, D_MODEL)),
            jnp.broadcast_to(hst_m, (B, 1, D_MODEL)), *rnn_w, bt=B, t_len=SEQ, tt=tt_f)
        oaf = _attn_frames(zf, km_pad, vm_pad, **lam_kw, bt=B, t_len=SEQ, tq=256, tkb=1024,
                           n_extra=N_META, lam_init=p["lam_init"])
        hf = _merge(hf, orf, oaf, zf, p["wpr"], p["wpa"], p["wo"], fnw, tm=tm_f, final=final)

        kvf3 = kvf.reshape(B, SEQ, 2 * D_MODEL)
        kvm3 = jnp.broadcast_to(kvm[None], (B, N_META, 2 * D_MODEL))
        kp.append(jnp.concatenate([kvm3[..., :D_MODEL], kvf3[..., :D_MODEL]], axis=1))
        vp.append(jnp.concatenate([kvm3[..., D_MODEL:], kvf3[..., D_MODEL:]], axis=1))
        cp.append(cst_f[:, SUBLANES - (CONV_W - 1):, :])
        rp.append(hst_f[:, 0, :])

        zs, kvs = _inproj(hs, p["nw"], p["w_in"], cos_s, sin_s, tm=tm_s)
        pre_s = jnp.pad(state_conv[l], ((0, 0), (SUBLANES - (CONV_W - 1), 0), (0, 0)))
        ors, cst_s, hst_s = _rglru(zs, pre_s, state_rnn[l][:, None, :], *rnn_w, bt=DB, t_len=S, tt=S)
        pad_own = lambda a: jnp.pad(a.reshape(DB, S, D_MODEL), ((0, 0), (0, LANES - S), (0, 0))
                                    ).reshape(DB * LANES, D_MODEL)
        oas = _attn_small(zs, pad_own(kcols(zs)), pad_own(vcols(zs)), (cache_k4, cache_v4),
                          **lam_kw, bt=DB, tq=S, n_own=S, layer=l, lam_init=p["lam_init"])
        hs = _merge(hs, ors, oas, zs, p["wpr"], p["wpa"], p["wo"], fnw, tm=tm_s, final=final)

        kvs3 = kvs.reshape(DB, S, 2 * D_MODEL)
        ks.append(kvs3[..., :D_MODEL])
        vs.append(kvs3[..., D_MODEL:])
        cs.append(cst_s[:, SUBLANES - (CONV_W - 1):, :])
        rs.append(hst_s[:, 0, :])

    tp = N_META + SEQ
    return (
        hf.reshape(B, SEQ, D_MODEL),
        hs.reshape(DB, S, D_MODEL),
        jnp.stack(kp).reshape(depth, B, tp, N_HEADS, 2, HEAD_DIM),
        jnp.stack(vp).reshape(depth, B, tp, N_HEADS, V_DIM),
        jnp.stack(cp),
        jnp.stack(rp),
        jnp.stack(ks).reshape(depth, DB, S, N_HEADS, 2, HEAD_DIM),
        jnp.stack(vs).reshape(depth, DB, S, N_HEADS, V_DIM),
        jnp.stack(cs),
        jnp.stack(rs),
    )
```

```python
import functools
import math

import jax
import jax.numpy as jnp
from jax import lax
from jax.experimental import pallas as pl
from jax.experimental.pallas import tpu as pltpu

F32 = jnp.float32
BF16 = jnp.bfloat16

D_MODEL = 1024
N_HEADS = 8
HEAD_DIM = 64
V_DIM = 128
CHUNK = 64
CHUNK_SHIFT = 6
N_META = 16
CONV_W = 4
LRU_C = 8.0
N_RNN_BLOCKS = 8
RNN_BLOCK = 128
ROPE_THETA = 10000.0
EPS = 1e-6
N_GROUPS = 8
LANES = 128
SUBLANES = 8
G_XR, G_ZR, G_Q, G_K, G_V, G_ZA, G_GR, G_GA = range(8)
Q_SCALE = (HEAD_DIM ** -0.5) * math.log2(math.e)
NEG_BIG = -1e30
ONES_ROWS = 16
KEY_BLOCK = 512


def _sigmoid(x):
    return 0.5 * jnp.tanh(0.5 * x) + 0.5
VMEM_LIMIT = 56 * 1024 * 1024


def _sds(shape, dtype):
    return jax.ShapeDtypeStruct(shape, dtype)


def _nt_dot(a, b):
    return lax.dot_general(a, b, (((1,), (1,)), ((), ())), preferred_element_type=F32)


def _inproj_body(x_ref, nw_ref, w_ref, cos_ref, sin_ref, z_ref, kv_ref, *rest, tm, emit_vt):
    if emit_vt:
        vt_ref, xn_ref = rest
    else:
        (xn_ref,) = rest
    j = pl.program_id(1)

    @pl.when(j == 0)
    def _():
        x = x_ref[...]
        ms = jnp.mean(x * x, axis=-1, keepdims=True)
        xn_ref[...] = (x * lax.rsqrt(ms + EPS) * nw_ref[...]).astype(BF16)

    acc = jnp.dot(xn_ref[...], w_ref[...], preferred_element_type=F32)

    def rope_slab(s, first_half):
        rot = jnp.where(first_half, pltpu.roll(s, LANES - 32, 1), pltpu.roll(s, 32, 1))
        return s * cos_ref[...] + rot * sin_ref[...]

    @pl.when(j == G_XR)
    def _():
        z_ref[...] = acc.astype(BF16)

    @pl.when((j == G_ZR) | (j == G_ZA))
    def _():
        z_ref[...] = (acc * _sigmoid(acc)).astype(BF16)

    @pl.when(j == G_Q)
    def _():
        first_half = (lax.broadcasted_iota(jnp.int32, (tm, LANES), 1) & 63) < 32
        for hh in range(N_HEADS):
            sl = slice(hh * LANES, (hh + 1) * LANES)
            z_ref[:, sl] = (rope_slab(acc[:, sl], first_half) * Q_SCALE).astype(BF16)

    @pl.when(j == G_K)
    def _():
        first_half = (lax.broadcasted_iota(jnp.int32, (tm, LANES), 1) & 63) < 32
        for hh in range(N_HEADS):
            sl = slice(hh * LANES, (hh + 1) * LANES)
            r = rope_slab(acc[:, sl], first_half)
            kv_ref[:, sl] = r
            z_ref[:, sl] = r.astype(BF16)

    @pl.when(j == G_V)
    def _():
        kv_ref[...] = acc
        z_ref[...] = acc.astype(BF16)
        if emit_vt:
            for kb in range(tm // KEY_BLOCK):
                vt_ref[kb] = acc[kb * KEY_BLOCK:(kb + 1) * KEY_BLOCK, :].T.astype(BF16)

    @pl.when(j >= G_GR)
    def _():
        z_ref[...] = _sigmoid(acc).astype(BF16)


def _inproj(h2d, nw, w_bf, cos, sin, *, tm, emit_vt=False):
    n = h2d.shape[0]
    ntab = cos.shape[0] // tm

    def kv_index(i, j):
        return (i, jnp.where(j > G_K, 1, 0))

    out_specs = [
        pl.BlockSpec((tm, D_MODEL), lambda i, j: (i, j)),
        pl.BlockSpec((tm, D_MODEL), kv_index),
    ]
    out_shape = [_sds((n, N_GROUPS * D_MODEL), BF16), _sds((n, 2 * D_MODEL), F32)]
    if emit_vt:
        out_specs.append(pl.BlockSpec((tm // KEY_BLOCK, D_MODEL, KEY_BLOCK), lambda i, j: (i, 0, 0)))
        out_shape.append(_sds((n // KEY_BLOCK, D_MODEL, KEY_BLOCK), BF16))

    return pl.pallas_call(
        functools.partial(_inproj_body, tm=tm, emit_vt=emit_vt),
        grid=(n // tm, N_GROUPS),
        in_specs=[
            pl.BlockSpec((tm, D_MODEL), lambda i, j: (i, 0)),
            pl.BlockSpec((1, D_MODEL), lambda i, j: (0, 0)),
            pl.BlockSpec((D_MODEL, D_MODEL), lambda i, j: (0, j)),
            pl.BlockSpec((tm, LANES), lambda i, j: (i % ntab, 0)),
            pl.BlockSpec((tm, LANES), lambda i, j: (i % ntab, 0)),
        ],
        out_specs=out_specs,
        out_shape=out_shape,
        scratch_shapes=[pltpu.VMEM((tm, D_MODEL), BF16)],
        compiler_params=pltpu.CompilerParams(
            dimension_semantics=("arbitrary", "arbitrary"), vmem_limit_bytes=VMEM_LIMIT),
        name="inproj",
    )(h2d, nw, w_bf, cos, sin)


def _rglru_body(xr_ref, sz_ref, pre_ref, h0_ref, cw_ref, cb_ref, wg_ref, brg_ref, big_ref, lam_ref,
                o_ref, cst_ref, hst_ref, xbuf, a_s, b_s, h_s, hc, *, tt):
    t = pl.program_id(1)

    @pl.when(t == 0)
    def _():
        xbuf[0:SUBLANES, :] = pre_ref[0]
        hc[...] = jnp.broadcast_to(h0_ref[0], (SUBLANES, D_MODEL))

    xr = xr_ref[...].astype(F32)
    xbuf[SUBLANES:SUBLANES + tt, :] = xr
    cw = cw_ref[...]
    xc = cb_ref[...] + xbuf[SUBLANES - 3:SUBLANES - 3 + tt, :] * cw[0:1, :]
    xc = xc + xbuf[SUBLANES - 2:SUBLANES - 2 + tt, :] * cw[1:2, :]
    xc = xc + xbuf[SUBLANES - 1:SUBLANES - 1 + tt, :] * cw[2:3, :]
    xc = xc + xr * cw[3:4, :]
    xcb = xc.astype(BF16)

    nl = -lam_ref[...]
    softplus = jnp.maximum(nl, 0.0) + jnp.log(1.0 + jnp.exp(-jnp.abs(nl)))
    c = -LRU_C * softplus

    rowmod = lax.broadcasted_iota(jnp.int32, (tt, RNN_BLOCK), 0) & (SUBLANES - 1)
    for n in range(N_RNN_BLOCKS):
        sl = slice(n * RNN_BLOCK, (n + 1) * RNN_BLOCK)
        g = jnp.dot(xcb[:, sl], wg_ref[n], preferred_element_type=F32)
        r = _sigmoid(g[:, :RNN_BLOCK] + brg_ref[:, sl])
        ig = _sigmoid(g[:, RNN_BLOCK:] + big_ref[:, sl])
        log_a = c[:, sl] * r
        a = jnp.exp(log_a)
        mult = jnp.sqrt(-jnp.tanh(log_a) * (1.0 + a * a))
        bb = mult * ig * xc[:, sl]
        for d in (1, 2, 4):
            keep = rowmod >= d
            a_p = jnp.where(keep, pltpu.roll(a, d, 0), 1.0)
            b_p = jnp.where(keep, pltpu.roll(bb, d, 0), 0.0)
            bb = bb + a * b_p
            a = a * a_p
        a_s[:, sl] = a
        b_s[:, sl] = bb

    def group(gi, h):
        rows = pl.ds(pl.multiple_of(gi * SUBLANES, SUBLANES), SUBLANES)
        hb = a_s[rows, :] * h + b_s[rows, :]
        h_s[rows, :] = hb
        return jnp.broadcast_to(hb[SUBLANES - 1:SUBLANES, :], (SUBLANES, D_MODEL))

    h = lax.fori_loop(0, tt // SUBLANES, group, hc[...])
    hc[...] = h
    o_ref[...] = (h_s[...] * sz_ref[...].astype(F32)).astype(BF16)
    hst_ref[0] = h[0:1, :]
    tail = xbuf[tt:tt + SUBLANES, :]
    cst_ref[0] = tail
    xbuf[0:SUBLANES, :] = tail


def _rglru(z, prefix, h0, cw, cb, wg, brg, big, lam, *, bt, t_len, tt):
    nt = t_len // tt
    n = bt * t_len
    vec = lambda: pl.BlockSpec((1, D_MODEL), lambda b, t: (0, 0))
    return pl.pallas_call(
        functools.partial(_rglru_body, tt=tt),
        grid=(bt, nt),
        in_specs=[
            pl.BlockSpec((tt, D_MODEL), lambda b, t: (b * nt + t, G_XR)),
            pl.BlockSpec((tt, D_MODEL), lambda b, t: (b * nt + t, G_ZR)),
            pl.BlockSpec((1, SUBLANES, D_MODEL), lambda b, t: (b, 0, 0)),
            pl.BlockSpec((1, 1, D_MODEL), lambda b, t: (b, 0, 0)),
            pl.BlockSpec((CONV_W, D_MODEL), lambda b, t: (0, 0)),
            vec(),
            pl.BlockSpec((N_RNN_BLOCKS, RNN_BLOCK, 2 * RNN_BLOCK), lambda b, t: (0, 0, 0)),
            vec(), vec(), vec(),
        ],
        out_specs=[
            pl.BlockSpec((tt, D_MODEL), lambda b, t: (b * nt + t, 0)),
            pl.BlockSpec((1, SUBLANES, D_MODEL), lambda b, t: (b, 0, 0)),
            pl.BlockSpec((1, 1, D_MODEL), lambda b, t: (b, 0, 0)),
        ],
        out_shape=[_sds((n, D_MODEL), BF16), _sds((bt, SUBLANES, D_MODEL), F32),
                   _sds((bt, 1, D_MODEL), F32)],
        scratch_shapes=[
            pltpu.VMEM((tt + SUBLANES, D_MODEL), F32),
            pltpu.VMEM((tt, D_MODEL), F32),
            pltpu.VMEM((tt, D_MODEL), F32),
            pltpu.VMEM((tt, D_MODEL), F32),
            pltpu.VMEM((SUBLANES, D_MODEL), F32),
        ],
        compiler_params=pltpu.CompilerParams(
            dimension_semantics=("arbitrary", "arbitrary"), vmem_limit_bytes=VMEM_LIMIT),
        name="rglru",
    )(z, z, prefix, h0, cw, cb, wg, brg, big, lam)


def _stack_maps(q):
    lane = lax.broadcasted_iota(jnp.int32, q.shape, 1)
    zero = jnp.zeros_like(q)
    return jnp.concatenate(
        [jnp.where(lane < HEAD_DIM, q, zero), jnp.where(lane >= HEAD_DIM, q, zero)], axis=0)


def _softmax_update(pairs, m_s, l_s, acc_s):
    m_prev = m_s[...]
    m_new = m_prev
    for s, _ in pairs:
        m_new = jnp.maximum(m_new, jnp.max(s, axis=-1, keepdims=True))
    alpha = jnp.exp2(m_prev - m_new)
    l_new = alpha * l_s[...]
    acc = alpha * acc_s[...]
    for s, v in pairs:
        p = jnp.exp2(s - m_new)
        l_new = l_new + jnp.sum(p, axis=-1, keepdims=True)
        acc = acc + jnp.dot(p.astype(BF16), v, preferred_element_type=F32)
    m_s[...] = m_new
    l_s[...] = l_new
    acc_s[...] = acc


def _attn_finish(acc, l, sza, subw, lq1, lk1, lq2, lk2, tq, lam_init):
    o = acc * (1.0 / l)
    lam = (jnp.exp(jnp.sum(lq1 * lk1, axis=-1, keepdims=True))
           - jnp.exp(jnp.sum(lq2 * lk2, axis=-1, keepdims=True)) + lam_init)
    od = o[:tq, :] - lam * o[tq:, :]
    ms = jnp.mean(od * od, axis=-1, keepdims=True)
    on = (od * lax.rsqrt(ms + EPS) * subw) * (1.0 - lam_init)
    return (on * sza.astype(F32)).astype(BF16)


def _attn_frames_body(q_ref, k_ref, vt_ref, km_ref, vmt_ref, sza_ref, subc_ref, lq1_ref, lk1_ref,
                      lq2_ref, lk2_ref, o_ref, s_buf0, s_buf1, p_buf0, p_buf1, mx_buf0, mx_buf1, pm_s,
                      m_s, acc_s, *, tq, tk, nk, n_extra, lam_init):
    i = pl.program_id(2)
    n_blocks = (i * tq) // tk + 1
    s_bufs, p_bufs, mx_bufs = (s_buf0, s_buf1), (p_buf0, p_buf1), (mx_buf0, mx_buf1)

    qt = q_ref[...].astype(F32).T
    rowq = lax.broadcasted_iota(jnp.int32, qt.shape, 0)
    q2t = jnp.concatenate([jnp.where(rowq < HEAD_DIM, qt, 0.0), jnp.where(rowq >= HEAD_DIM, qt, 0.0)],
                          axis=1).astype(BF16)
    m_s[...] = jnp.full(m_s.shape, NEG_BIG, F32)
    acc_s[...] = jnp.zeros(acc_s.shape, F32)

    def with_ones(vt):
        return jnp.concatenate([vt, jnp.ones((ONES_ROWS, vt.shape[1]), BF16)], axis=0)

    def stage_a(j):
        st = jnp.dot(k_ref[j * tk:(j + 1) * tk, :], q2t, preferred_element_type=F32)
        s_bufs[j % 2][...] = st
        mx_bufs[j % 2][...] = jnp.max(st, axis=0, keepdims=True)

    def stage_b(j, diagonal):
        st = s_bufs[j % 2][...]
        m_prev = m_s[...]
        if diagonal:
            row = lax.broadcasted_iota(jnp.int32, st.shape, 0)
            col = lax.broadcasted_iota(jnp.int32, st.shape, 1)
            visible = ((j * tk + row) >> CHUNK_SHIFT) <= ((i * tq + (col & (tq - 1))) >> CHUNK_SHIFT)
            st = jnp.where(visible, st, -jnp.inf)
            sm = jnp.dot(km_ref[...], q2t, preferred_element_type=F32)
            rowm = lax.broadcasted_iota(jnp.int32, sm.shape, 0)
            sm = jnp.where(rowm < n_extra, sm, -jnp.inf)
            m_new = jnp.maximum(jnp.maximum(m_prev, jnp.max(st, axis=0, keepdims=True)),
                                jnp.max(sm, axis=0, keepdims=True))
            pm_s[...] = jnp.exp2(sm - m_new).astype(BF16)
        else:
            m_new = jnp.maximum(m_prev, mx_bufs[j % 2][...])
        p_bufs[j % 2][...] = jnp.exp2(st - m_new).astype(BF16)
        m_s[...] = m_new
        return jnp.exp2(m_prev - m_new)

    def stage_c(j, alpha, last=False):
        u = jnp.dot(with_ones(vt_ref[j]), p_bufs[j % 2][...], preferred_element_type=F32)
        if last:
            u = u + jnp.dot(with_ones(vmt_ref[...]), pm_s[...], preferred_element_type=F32)
        acc = acc_s[...] + u
        acc_s[...] = acc if alpha is None else alpha * acc

    stage_a(0)
    for t in range(1, nk + 2):
        if t <= nk - 1:
            @pl.when(t <= n_blocks - 1)
            def _(t=t):
                alpha = stage_b(t - 1, False)
                if t >= 2:
                    stage_c(t - 2, alpha)
                stage_a(t)

        if t <= nk:
            @pl.when(t == n_blocks)
            def _(t=t):
                alpha = stage_b(t - 1, True)
                if t >= 2:
                    stage_c(t - 2, alpha)

        if t >= 2:
            @pl.when(t == n_blocks + 1)
            def _(t=t):
                stage_c(t - 2, None, last=True)

    lam = (jnp.exp(jnp.sum(lq1_ref[...] * lk1_ref[...], axis=-1, keepdims=True))
           - jnp.exp(jnp.sum(lq2_ref[...] * lk2_ref[...], axis=-1, keepdims=True)) + lam_init)
    ot = acc_s[:V_DIM, :] * (1.0 / acc_s[V_DIM:V_DIM + 1, :])
    od = ot[:, :tq] - lam * ot[:, tq:]
    ms = jnp.mean(od * od, axis=0, keepdims=True)
    on = (od * lax.rsqrt(ms + EPS) * subc_ref[...]) * (1.0 - lam_init)
    o_ref[...] = (on.T * sza_ref[...].astype(F32)).astype(BF16)


def _attn_frames(z, vt, km, vmt, subc, lq1, lk1, lq2, lk2, *, bt, t_len, tq, n_extra, lam_init):
    tk = vt.shape[2]
    nq, nk = t_len // tq, t_len // tk
    n = bt * t_len
    hcol = lambda g: (g * D_MODEL) // LANES
    small = lambda c: pl.BlockSpec((1, c), lambda b, h, i: (0, 0))
    return pl.pallas_call(
        functools.partial(_attn_frames_body, tq=tq, tk=tk, nk=nk, n_extra=n_extra,
                          lam_init=lam_init),
        grid=(bt, N_HEADS, nq),
        in_specs=[
            pl.BlockSpec((tq, LANES), lambda b, h, i: (b * nq + i, hcol(G_Q) + h)),
            pl.BlockSpec((t_len, LANES), lambda b, h, i: (b, hcol(G_K) + h)),
            pl.BlockSpec((nk, LANES, tk), lambda b, h, i: (b, h, 0)),
            pl.BlockSpec((LANES, LANES), lambda b, h, i: (0, h)),
            pl.BlockSpec((LANES, LANES), lambda b, h, i: (h, 0)),
            pl.BlockSpec((tq, LANES), lambda b, h, i: (b * nq + i, hcol(G_ZA) + h)),
            pl.BlockSpec((V_DIM, 1), lambda b, h, i: (0, 0)),
            small(HEAD_DIM), small(HEAD_DIM), small(HEAD_DIM), small(HEAD_DIM),
        ],
        out_specs=pl.BlockSpec((tq, LANES), lambda b, h, i: (b * nq + i, h)),
        out_shape=_sds((n, D_MODEL), BF16),
        scratch_shapes=[
            pltpu.VMEM((tk, 2 * tq), F32),
            pltpu.VMEM((tk, 2 * tq), F32),
            pltpu.VMEM((tk, 2 * tq), BF16),
            pltpu.VMEM((tk, 2 * tq), BF16),
            pltpu.VMEM((1, 2 * tq), F32),
            pltpu.VMEM((1, 2 * tq), F32),
            pltpu.VMEM((LANES, 2 * tq), BF16),
            pltpu.VMEM((1, 2 * tq), F32),
            pltpu.VMEM((V_DIM + ONES_ROWS, 2 * tq), F32),
        ],
        compiler_params=pltpu.CompilerParams(
            dimension_semantics=("arbitrary", "arbitrary", "arbitrary"),
            vmem_limit_bytes=VMEM_LIMIT),
        name="attn_frames",
    )(z, z, vt, km, vmt, z, subc, lq1, lk1, lq2, lk2)


def _attn_small_body(*refs, tq, n_own, has_cache, lam_init):
    if has_cache:
        (q_ref, kc_ref, vc_ref, ko_ref, vo_ref, sza_ref, sub_ref, lq1_ref, lk1_ref, lq2_ref,
         lk2_ref, o_ref, m_s, l_s, acc_s) = refs
    else:
        (q_ref, ko_ref, vo_ref, sza_ref, sub_ref, lq1_ref, lk1_ref, lq2_ref,
         lk2_ref, o_ref, m_s, l_s, acc_s) = refs
    q2 = _stack_maps(q_ref[...])
    m_s[...] = jnp.full(m_s.shape, NEG_BIG, F32)
    l_s[...] = jnp.zeros(l_s.shape, F32)
    acc_s[...] = jnp.zeros(acc_s.shape, F32)
    pairs = []
    if has_cache:
        pairs.append((_nt_dot(q2, kc_ref[0, 0].astype(BF16)), vc_ref[0, 0].astype(BF16)))
    so = _nt_dot(q2, ko_ref[...])
    colo = lax.broadcasted_iota(jnp.int32, so.shape, 1)
    so = jnp.where(colo < n_own, so, -jnp.inf)
    pairs.append((so, vo_ref[...]))
    _softmax_update(pairs, m_s, l_s, acc_s)
    o_ref[...] = _attn_finish(acc_s[...], l_s[...], sza_ref[...], sub_ref[...], lq1_ref[...],
                              lk1_ref[...], lq2_ref[...], lk2_ref[...], tq, lam_init)


def _attn_small(z, ko, vo, cache, subw, lq1, lk1, lq2, lk2, *, bt, tq, n_own, layer, lam_init):
    hcol = lambda g: (g * D_MODEL) // LANES
    small = lambda w: pl.BlockSpec((1, w), lambda b, h: (0, 0))
    has_cache = cache is not None
    in_specs = [pl.BlockSpec((tq, LANES), lambda b, h: (b, hcol(G_Q) + h))]
    args = [z]
    if has_cache:
        past = cache[0].shape[2]
        in_specs += [pl.BlockSpec((1, 1, past, LANES), lambda b, h: (layer, b, 0, h))] * 2
        args += list(cache)
    in_specs += [
        pl.BlockSpec((LANES, LANES), lambda b, h: (b, h)),
        pl.BlockSpec((LANES, LANES), lambda b, h: (b, h)),
        pl.BlockSpec((tq, LANES), lambda b, h: (b, hcol(G_ZA) + h)),
        small(V_DIM), small(HEAD_DIM), small(HEAD_DIM), small(HEAD_DIM), small(HEAD_DIM),
    ]
    args += [ko, vo, z, subw, lq1, lk1, lq2, lk2]
    return pl.pallas_call(
        functools.partial(_attn_small_body, tq=tq, n_own=n_own, has_cache=has_cache,
                          lam_init=lam_init),
        grid=(bt, N_HEADS),
        in_specs=in_specs,
        out_specs=pl.BlockSpec((tq, LANES), lambda b, h: (b, h)),
        out_shape=_sds((bt * tq, D_MODEL), BF16),
        scratch_shapes=[
            pltpu.VMEM((2 * tq, 1), F32),
            pltpu.VMEM((2 * tq, 1), F32),
            pltpu.VMEM((2 * tq, V_DIM), F32),
        ],
        compiler_params=pltpu.CompilerParams(
            dimension_semantics=("arbitrary", "arbitrary"), vmem_limit_bytes=VMEM_LIMIT),
        name="attn_small",
    )(*args)


def _merge_body(h_ref, or_ref, oa_ref, gr_ref, ga_ref, wpr_ref, wpa_ref, wo_ref, fnw_ref, out_ref,
                *, final):
    pr = jnp.dot(or_ref[...], wpr_ref[...], preferred_element_type=F32)
    pa = jnp.dot(oa_ref[...], wpa_ref[...], preferred_element_type=F32)
    m = gr_ref[...].astype(F32) * pr + ga_ref[...].astype(F32) * pa
    hn = h_ref[...] + jnp.dot(m.astype(BF16), wo_ref[...], preferred_element_type=F32)
    if final:
        ms = jnp.mean(hn * hn, axis=-1, keepdims=True)
        hn = hn * lax.rsqrt(ms + EPS) * fnw_ref[...]
    out_ref[...] = hn


def _merge(h2d, o_r, o_a, z, wpr, wpa, wo, fnw, *, tm, final):
    n = h2d.shape[0]
    tile = lambda: pl.BlockSpec((tm, D_MODEL), lambda i: (i, 0))
    wspec = lambda: pl.BlockSpec((D_MODEL, D_MODEL), lambda i: (0, 0))
    return pl.pallas_call(
        functools.partial(_merge_body, final=final),
        grid=(n // tm,),
        in_specs=[
            tile(), tile(), tile(),
            pl.BlockSpec((tm, D_MODEL), lambda i: (i, G_GR)),
            pl.BlockSpec((tm, D_MODEL), lambda i: (i, G_GA)),
            wspec(), wspec(), wspec(),
            pl.BlockSpec((1, D_MODEL), lambda i: (0, 0)),
        ],
        out_specs=tile(),
        out_shape=_sds((n, D_MODEL), F32),
        compiler_params=pltpu.CompilerParams(
            dimension_semantics=("arbitrary",), vmem_limit_bytes=VMEM_LIMIT),
        name="merge_out",
    )(h2d, o_r, o_a, z, z, wpr, wpa, wo, fnw)


def _rope_tables(pos):
    half = HEAD_DIM // 2
    inv = 1.0 / (ROPE_THETA ** (jnp.arange(half, dtype=F32) / half))
    ang = pos.astype(F32)[:, None] * inv[None, :]
    cos, sin = jnp.cos(ang), jnp.sin(ang)
    cos_t = jnp.concatenate([cos, cos, cos, cos], axis=1)
    sin_t = jnp.concatenate([-sin, sin, -sin, sin], axis=1)
    return cos_t, sin_t


def _pick_tile(n, pref):
    t = pref
    while n % t:
        t //= 2
    return t


def _pad_rows(x, rows):
    return jnp.pad(x, ((0, rows - x.shape[0]), (0, 0)))


def kernel(x_prompt, x_sample, cache_k, cache_v, state_conv, state_rnn, meta_tokens, norm_w, w_in,
           conv_w, conv_b, w_rg, b_rg, w_ig, b_ig, lru_lambda, lambda_q1, lambda_k1, lambda_q2,
           lambda_k2, subln_w, w_proj_rnn, w_proj_att, w_out, final_norm_w):
    B, SEQ, _ = x_prompt.shape
    DB, S, _ = x_sample.shape
    depth, _, past = cache_k.shape[0], cache_k.shape[1], cache_k.shape[2]
    assert SEQ % 1024 == 0 and S % 16 == 0 and S <= LANES

    w_in_bf = w_in.astype(BF16)
    wg_bf = jnp.concatenate([w_rg, w_ig], axis=-1).astype(BF16)
    wpr_bf, wpa_bf, wo_bf = w_proj_rnn.astype(BF16), w_proj_att.astype(BF16), w_out.astype(BF16)
    cache_k4 = cache_k.reshape(depth, DB, past, D_MODEL)
    cache_v4 = cache_v.reshape(depth, DB, past, D_MODEL)
    fnw = final_norm_w.reshape(1, D_MODEL)

    cos_m, sin_m = _rope_tables(jnp.arange(N_META))
    cos_f, sin_f = _rope_tables(N_META + jnp.arange(SEQ))
    cos_s, sin_s = _rope_tables(N_META + past + jnp.arange(S))
    cos_s, sin_s = jnp.tile(cos_s, (DB, 1)), jnp.tile(sin_s, (DB, 1))

    tm_f = _pick_tile(SEQ, 1024)
    tm_s = _pick_tile(DB * S, 512)
    tt_f = _pick_tile(SEQ, 512)

    def layer_params(l):
        row = lambda a: a[l].reshape(1, -1)
        return dict(
            nw=row(norm_w), w_in=w_in_bf[l], cw=conv_w[l], cb=row(conv_b), wg=wg_bf[l],
            brg=row(b_rg), big=row(b_ig), lam=row(lru_lambda), subw=row(subln_w),
            lq1=row(lambda_q1), lk1=row(lambda_k1), lq2=row(lambda_q2), lk2=row(lambda_k2),
            wpr=wpr_bf[l], wpa=wpa_bf[l], wo=wo_bf[l],
            lam_init=0.8 - 0.6 * math.exp(-0.3 * l))

    def kcols(z):
        return z[:, G_K * D_MODEL:(G_K + 1) * D_MODEL]

    def vcols(z):
        return z[:, G_V * D_MODEL:(G_V + 1) * D_MODEL]

    hm = meta_tokens.astype(F32)
    hf = x_prompt.reshape(B * SEQ, D_MODEL)
    hs = x_sample.reshape(DB * S, D_MODEL)
    zero_prefix = jnp.zeros((1, SUBLANES, D_MODEL), F32)
    zero_h0 = jnp.zeros((1, 1, D_MODEL), F32)

    kp, vp, cp, rp, ks, vs, cs, rs = [], [], [], [], [], [], [], []
    for l in range(depth):
        p = layer_params(l)
        final = l == depth - 1
        lam_kw = dict(subw=p["subw"], lq1=p["lq1"], lk1=p["lk1"], lq2=p["lq2"], lk2=p["lk2"])
        rnn_w = (p["cw"], p["cb"], p["wg"], p["brg"], p["big"], p["lam"])

        zm, kvm = _inproj(hm, p["nw"], p["w_in"], cos_m, sin_m, tm=N_META)
        subc = p["subw"].reshape(V_DIM, 1)
        orm, cst_m, hst_m = _rglru(zm, zero_prefix, zero_h0, *rnn_w, bt=1, t_len=N_META, tt=N_META)
        km_pad, vm_pad = _pad_rows(kcols(zm), LANES), _pad_rows(vcols(zm), LANES)
        oam = _attn_small(zm, km_pad, vm_pad, None, **lam_kw, bt=1, tq=N_META, n_own=N_META,
                          layer=l, lam_init=p["lam_init"])
        hm = _merge(hm, orm, oam, zm, p["wpr"], p["wpa"], p["wo"], fnw, tm=N_META, final=False)

        zf, kvf, vtf = _inproj(hf, p["nw"], p["w_in"], cos_f, sin_f, tm=tm_f, emit_vt=True)
        orf, cst_f, hst_f = _rglru(
            zf, jnp.broadcast_to(cst_m, (B, SUBLANES, D_MODEL)),
            jnp.broadcast_to(hst_m, (B, 1, D_MODEL)), *rnn_w, bt=B, t_len=SEQ, tt=tt_f)
        oaf = _attn_frames(zf, vtf, km_pad, vm_pad.T, subc, p["lq1"], p["lk1"], p["lq2"], p["lk2"],
                           bt=B, t_len=SEQ, tq=512, n_extra=N_META, lam_init=p["lam_init"])
        hf = _merge(hf, orf, oaf, zf, p["wpr"], p["wpa"], p["wo"], fnw, tm=tt_f, final=final)

        kvf3 = kvf.reshape(B, SEQ, 2 * D_MODEL)
        kvm3 = jnp.broadcast_to(kvm[None], (B, N_META, 2 * D_MODEL))
        kp.append(jnp.concatenate([kvm3[..., :D_MODEL], kvf3[..., :D_MODEL]], axis=1))
        vp.append(jnp.concatenate([kvm3[..., D_MODEL:], kvf3[..., D_MODEL:]], axis=1))
        cp.append(cst_f[:, SUBLANES - (CONV_W - 1):, :])
        rp.append(hst_f[:, 0, :])

        zs, kvs = _inproj(hs, p["nw"], p["w_in"], cos_s, sin_s, tm=tm_s)
        pre_s = jnp.pad(state_conv[l], ((0, 0), (SUBLANES - (CONV_W - 1), 0), (0, 0)))
        ors, cst_s, hst_s = _rglru(zs, pre_s, state_rnn[l][:, None, :], *rnn_w, bt=DB, t_len=S, tt=S)
        pad_own = lambda a: jnp.pad(a.reshape(DB, S, D_MODEL), ((0, 0), (0, LANES - S), (0, 0))
                                    ).reshape(DB * LANES, D_MODEL)
        oas = _attn_small(zs, pad_own(kcols(zs)), pad_own(vcols(zs)), (cache_k4, cache_v4),
                          **lam_kw, bt=DB, tq=S, n_own=S, layer=l, lam_init=p["lam_init"])
        hs = _merge(hs, ors, oas, zs, p["wpr"], p["wpa"], p["wo"], fnw, tm=tm_s, final=final)

        kvs3 = kvs.reshape(DB, S, 2 * D_MODEL)
        ks.append(kvs3[..., :D_MODEL])
        vs.append(kvs3[..., D_MODEL:])
        cs.append(cst_s[:, SUBLANES - (CONV_W - 1):, :])
        rs.append(hst_s[:, 0, :])

    tp = N_META + SEQ
    return (
        hf.reshape(B, SEQ, D_MODEL),
        hs.reshape(DB, S, D_MODEL),
        jnp.stack(kp).reshape(depth, B, tp, N_HEADS, 2, HEAD_DIM),
        jnp.stack(vp).reshape(depth, B, tp, N_HEADS, V_DIM),
        jnp.stack(cp),
        jnp.stack(rp),
        jnp.stack(ks).reshape(depth, DB, S, N_HEADS, 2, HEAD_DIM),
        jnp.stack(vs).reshape(depth, DB, S, N_HEADS, V_DIM),
        jnp.stack(cs),
        jnp.stack(rs),
    )
```

```python
import functools
import math

import jax
import jax.numpy as jnp
from jax import lax
from jax.experimental import pallas as pl
from jax.experimental.pallas import tpu as pltpu

F32 = jnp.float32
BF16 = jnp.bfloat16

D_MODEL = 1024
N_HEADS = 8
HEAD_DIM = 64
V_DIM = 128
CHUNK = 64
CHUNK_SHIFT = 6
N_META = 16
CONV_W = 4
LRU_C = 8.0
N_RNN_BLOCKS = 8
RNN_BLOCK = 128
ROPE_THETA = 10000.0
EPS = 1e-6
N_GROUPS = 8
LANES = 128
SUBLANES = 8
G_XR, G_ZR, G_Q, G_K, G_V, G_ZA, G_GR, G_GA = range(8)
Q_SCALE = (HEAD_DIM ** -0.5) * math.log2(math.e)
NEG_BIG = -1e30
ONES_ROWS = 16
KEY_BLOCK = 512


def _sigmoid(x):
    return 0.5 * jnp.tanh(0.5 * x) + 0.5
VMEM_LIMIT = 56 * 1024 * 1024


def _sds(shape, dtype):
    return jax.ShapeDtypeStruct(shape, dtype)


def _nt_dot(a, b):
    return lax.dot_general(a, b, (((1,), (1,)), ((), ())), preferred_element_type=F32)


def _inproj_body(*refs, tm, emit_vt, direct, aliased):
    x_ref, nw_ref, w_ref, cos_ref, sin_ref = refs[:5]
    refs = refs[5 + (2 if aliased else 0):]
    z_ref, refs = refs[0], refs[1:]
    if direct:
        k_ref, v_ref, refs = refs[0], refs[1], refs[2:]
    else:
        kv_ref, refs = refs[0], refs[1:]
    if emit_vt:
        vt_ref, refs = refs[0], refs[1:]
    (xn_ref,) = refs
    j = pl.program_id(1)

    @pl.when(j == 0)
    def _():
        x = x_ref[...]
        ms = jnp.mean(x * x, axis=-1, keepdims=True)
        xn_ref[...] = (x * lax.rsqrt(ms + EPS) * nw_ref[...]).astype(BF16)

    acc = jnp.dot(xn_ref[...], w_ref[...], preferred_element_type=F32)

    def rope_slab(s, first_half):
        rot = jnp.where(first_half, pltpu.roll(s, LANES - 32, 1), pltpu.roll(s, 32, 1))
        return s * cos_ref[...] + rot * sin_ref[...]

    @pl.when(j == G_XR)
    def _():
        z_ref[...] = acc.astype(BF16)

    @pl.when((j == G_ZR) | (j == G_ZA))
    def _():
        z_ref[...] = (acc * _sigmoid(acc)).astype(BF16)

    @pl.when(j == G_Q)
    def _():
        first_half = (lax.broadcasted_iota(jnp.int32, (tm, LANES), 1) & 63) < 32
        for hh in range(N_HEADS):
            sl = slice(hh * LANES, (hh + 1) * LANES)
            z_ref[:, sl] = (rope_slab(acc[:, sl], first_half) * Q_SCALE).astype(BF16)

    @pl.when(j == G_K)
    def _():
        first_half = (lax.broadcasted_iota(jnp.int32, (tm, LANES), 1) & 63) < 32
        for hh in range(N_HEADS):
            sl = slice(hh * LANES, (hh + 1) * LANES)
            r = rope_slab(acc[:, sl], first_half)
            if direct:
                k_ref[0, 0, :, sl] = r
            else:
                kv_ref[:, sl] = r
            z_ref[:, sl] = r.astype(BF16)

    @pl.when(j == G_V)
    def _():
        if direct:
            v_ref[0, 0] = acc
        else:
            kv_ref[...] = acc
        z_ref[...] = acc.astype(BF16)
        if emit_vt:
            for kb in range(tm // KEY_BLOCK):
                vt_ref[kb] = acc[kb * KEY_BLOCK:(kb + 1) * KEY_BLOCK, :].T.astype(BF16)

    @pl.when(j >= G_GR)
    def _():
        z_ref[...] = _sigmoid(acc).astype(BF16)


def _inproj(h2d, nw, w_bf, cos, sin, *, tm, emit_vt=False, direct=None):
    n = h2d.shape[0]
    ntab = cos.shape[0] // tm

    def kv_index(i, j):
        return (i, jnp.where(j > G_K, 1, 0))

    args = [h2d, nw, w_bf, cos, sin]
    in_specs = [
        pl.BlockSpec((tm, D_MODEL), lambda i, j: (i, 0)),
        pl.BlockSpec((1, D_MODEL), lambda i, j: (0, 0)),
        pl.BlockSpec((D_MODEL, D_MODEL), lambda i, j: (0, j)),
        pl.BlockSpec((tm, LANES), lambda i, j: (i % ntab, 0)),
        pl.BlockSpec((tm, LANES), lambda i, j: (i % ntab, 0)),
    ]
    out_specs = [pl.BlockSpec((tm, D_MODEL), lambda i, j: (i, j))]
    out_shape = [_sds((n, N_GROUPS * D_MODEL), BF16)]
    aliases = {}
    if direct is None:
        out_specs.append(pl.BlockSpec((tm, D_MODEL), kv_index))
        out_shape.append(_sds((n, 2 * D_MODEL), F32))
    else:
        layer, depth, bt, t_len, row0, k_buf, v_buf = direct
        nt = t_len // tm
        rows = pl.BlockSpec(
            (pl.Element(1), pl.Element(1), pl.Element(tm), pl.Element(D_MODEL)),
            lambda i, j: (layer, i // nt, pl.multiple_of(row0 + (i % nt) * tm, SUBLANES), 0))
        out_specs += [rows, rows]
        out_shape += [_sds((depth, bt, row0 + t_len, D_MODEL), F32)] * 2
        if k_buf is not None:
            in_specs += [pl.BlockSpec(memory_space=pl.ANY)] * 2
            args += [k_buf, v_buf]
            aliases = {5: 1, 6: 2}
    if emit_vt:
        out_specs.append(pl.BlockSpec((tm // KEY_BLOCK, D_MODEL, KEY_BLOCK), lambda i, j: (i, 0, 0)))
        out_shape.append(_sds((n // KEY_BLOCK, D_MODEL, KEY_BLOCK), BF16))

    return pl.pallas_call(
        functools.partial(_inproj_body, tm=tm, emit_vt=emit_vt, direct=direct is not None,
                          aliased=bool(aliases)),
        grid=(n // tm, N_GROUPS),
        in_specs=in_specs,
        out_specs=out_specs,
        out_shape=out_shape,
        input_output_aliases=aliases,
        scratch_shapes=[pltpu.VMEM((tm, D_MODEL), BF16)],
        compiler_params=pltpu.CompilerParams(
            dimension_semantics=("arbitrary", "arbitrary"), vmem_limit_bytes=VMEM_LIMIT),
        name="inproj",
    )(*args)


def _rglru_body(xr_ref, sz_ref, pre_ref, h0_ref, cw_ref, cb_ref, wg_ref, brg_ref, big_ref, lam_ref,
                o_ref, cst_ref, hst_ref, xbuf, a_s, b_s, h_s, hc, *, tt):
    t = pl.program_id(1)

    @pl.when(t == 0)
    def _():
        xbuf[0:SUBLANES, :] = pre_ref[0]
        hc[...] = jnp.broadcast_to(h0_ref[0], (SUBLANES, D_MODEL))

    xr = xr_ref[...].astype(F32)
    xbuf[SUBLANES:SUBLANES + tt, :] = xr
    cw = cw_ref[...]
    xc = cb_ref[...] + xbuf[SUBLANES - 3:SUBLANES - 3 + tt, :] * cw[0:1, :]
    xc = xc + xbuf[SUBLANES - 2:SUBLANES - 2 + tt, :] * cw[1:2, :]
    xc = xc + xbuf[SUBLANES - 1:SUBLANES - 1 + tt, :] * cw[2:3, :]
    xc = xc + xr * cw[3:4, :]
    xcb = xc.astype(BF16)

    nl = -lam_ref[...]
    softplus = jnp.maximum(nl, 0.0) + jnp.log(1.0 + jnp.exp(-jnp.abs(nl)))
    c = -LRU_C * softplus

    grouped = (tt // SUBLANES, SUBLANES, RNN_BLOCK)
    rowmod = lax.broadcasted_iota(jnp.int32, grouped, 1)
    for n in range(N_RNN_BLOCKS):
        sl = slice(n * RNN_BLOCK, (n + 1) * RNN_BLOCK)
        g = jnp.dot(xcb[:, sl], wg_ref[n], preferred_element_type=F32)
        r = _sigmoid(g[:, :RNN_BLOCK] + brg_ref[:, sl])
        ig = _sigmoid(g[:, RNN_BLOCK:] + big_ref[:, sl])
        log_a = c[:, sl] * r
        a = jnp.exp(log_a)
        y = -jnp.tanh(log_a) * (1.0 + a * a)
        mult = jnp.where(y > 0.0, y * lax.rsqrt(y), 0.0)
        bb = (mult * ig * xc[:, sl]).reshape(grouped)
        a = a.reshape(grouped)
        for d in (1, 2, 4):
            keep = rowmod >= d
            a_p = jnp.where(keep, pltpu.roll(a, d, 1), 1.0)
            b_p = jnp.where(keep, pltpu.roll(bb, d, 1), 0.0)
            bb = bb + a * b_p
            a = a * a_p
        a_s[:, sl] = a.reshape(tt, RNN_BLOCK)
        b_s[:, sl] = bb.reshape(tt, RNN_BLOCK)

    def group(gi, h):
        rows = pl.ds(pl.multiple_of(gi * SUBLANES, SUBLANES), SUBLANES)
        hb = a_s[rows, :] * h + b_s[rows, :]
        h_s[rows, :] = hb
        return jnp.broadcast_to(hb[SUBLANES - 1:SUBLANES, :], (SUBLANES, D_MODEL))

    h = lax.fori_loop(0, tt // SUBLANES, group, hc[...])
    hc[...] = h
    o_ref[...] = (h_s[...] * sz_ref[...].astype(F32)).astype(BF16)
    hst_ref[0] = h[0:1, :]
    tail = xbuf[tt:tt + SUBLANES, :]
    cst_ref[0] = tail
    xbuf[0:SUBLANES, :] = tail


def _rglru(z, prefix, h0, cw, cb, wg, brg, big, lam, *, bt, t_len, tt):
    nt = t_len // tt
    n = bt * t_len
    vec = lambda: pl.BlockSpec((1, D_MODEL), lambda b, t: (0, 0))
    return pl.pallas_call(
        functools.partial(_rglru_body, tt=tt),
        grid=(bt, nt),
        in_specs=[
            pl.BlockSpec((tt, D_MODEL), lambda b, t: (b * nt + t, G_XR)),
            pl.BlockSpec((tt, D_MODEL), lambda b, t: (b * nt + t, G_ZR)),
            pl.BlockSpec((1, SUBLANES, D_MODEL), lambda b, t: (b, 0, 0)),
            pl.BlockSpec((1, 1, D_MODEL), lambda b, t: (b, 0, 0)),
            pl.BlockSpec((CONV_W, D_MODEL), lambda b, t: (0, 0)),
            vec(),
            pl.BlockSpec((N_RNN_BLOCKS, RNN_BLOCK, 2 * RNN_BLOCK), lambda b, t: (0, 0, 0)),
            vec(), vec(), vec(),
        ],
        out_specs=[
            pl.BlockSpec((tt, D_MODEL), lambda b, t: (b * nt + t, 0)),
            pl.BlockSpec((1, SUBLANES, D_MODEL), lambda b, t: (b, 0, 0)),
            pl.BlockSpec((1, 1, D_MODEL), lambda b, t: (b, 0, 0)),
        ],
        out_shape=[_sds((n, D_MODEL), BF16), _sds((bt, SUBLANES, D_MODEL), F32),
                   _sds((bt, 1, D_MODEL), F32)],
        scratch_shapes=[
            pltpu.VMEM((tt + SUBLANES, D_MODEL), F32),
            pltpu.VMEM((tt, D_MODEL), F32),
            pltpu.VMEM((tt, D_MODEL), F32),
            pltpu.VMEM((tt, D_MODEL), F32),
            pltpu.VMEM((SUBLANES, D_MODEL), F32),
        ],
        compiler_params=pltpu.CompilerParams(
            dimension_semantics=("arbitrary", "arbitrary"), vmem_limit_bytes=VMEM_LIMIT),
        name="rglru",
    )(z, z, prefix, h0, cw, cb, wg, brg, big, lam)


def _stack_maps(q):
    lane = lax.broadcasted_iota(jnp.int32, q.shape, 1)
    zero = jnp.zeros_like(q)
    return jnp.concatenate(
        [jnp.where(lane < HEAD_DIM, q, zero), jnp.where(lane >= HEAD_DIM, q, zero)], axis=0)


def _softmax_update(pairs, m_s, l_s, acc_s):
    m_prev = m_s[...]
    m_new = m_prev
    for s, _ in pairs:
        m_new = jnp.maximum(m_new, jnp.max(s, axis=-1, keepdims=True))
    alpha = jnp.exp2(m_prev - m_new)
    l_new = alpha * l_s[...]
    acc = alpha * acc_s[...]
    for s, v in pairs:
        p = jnp.exp2(s - m_new)
        l_new = l_new + jnp.sum(p, axis=-1, keepdims=True)
        acc = acc + jnp.dot(p.astype(BF16), v, preferred_element_type=F32)
    m_s[...] = m_new
    l_s[...] = l_new
    acc_s[...] = acc


def _attn_finish(acc, l, sza, subw, lq1, lk1, lq2, lk2, tq, lam_init):
    o = acc * (1.0 / l)
    lam = (jnp.exp(jnp.sum(lq1 * lk1, axis=-1, keepdims=True))
           - jnp.exp(jnp.sum(lq2 * lk2, axis=-1, keepdims=True)) + lam_init)
    od = o[:tq, :] - lam * o[tq:, :]
    ms = jnp.mean(od * od, axis=-1, keepdims=True)
    on = (od * lax.rsqrt(ms + EPS) * subw) * (1.0 - lam_init)
    return (on * sza.astype(F32)).astype(BF16)


def _attn_frames_body(q_ref, k_ref, vt_ref, km_ref, vmt_ref, bias_ref, sza_ref, subc_ref, lq1_ref,
                      lk1_ref, lq2_ref, lk2_ref, o_ref, s_buf0, s_buf1, p_buf0, p_buf1, mx_buf0, mx_buf1, pm_s,
                      m_s, acc_s, *, tq, tk, nk, n_extra, lam_init):
    i = pl.program_id(2)
    n_blocks = (i * tq) // tk + 1
    s_bufs, p_bufs, mx_bufs = (s_buf0, s_buf1), (p_buf0, p_buf1), (mx_buf0, mx_buf1)

    qt = q_ref[...].astype(F32).T
    rowq = lax.broadcasted_iota(jnp.int32, qt.shape, 0)
    q2t = jnp.concatenate([jnp.where(rowq < HEAD_DIM, qt, 0.0), jnp.where(rowq >= HEAD_DIM, qt, 0.0)],
                          axis=1).astype(BF16)
    m_s[...] = jnp.full(m_s.shape, NEG_BIG, F32)
    acc_s[...] = jnp.zeros(acc_s.shape, F32)

    def with_ones(vt):
        return jnp.concatenate([vt, jnp.ones((ONES_ROWS, vt.shape[1]), BF16)], axis=0)

    def stage_a(j):
        st = jnp.dot(k_ref[j * tk:(j + 1) * tk, :], q2t, preferred_element_type=F32)
        s_bufs[j % 2][...] = st
        mx_bufs[j % 2][...] = jnp.max(st, axis=0, keepdims=True)

    def stage_b(j, diagonal):
        st = s_bufs[j % 2][...]
        m_prev = m_s[...]
        if diagonal:
            st = st + bias_ref[...]
            sm = jnp.dot(km_ref[...], q2t, preferred_element_type=F32)
            rowm = lax.broadcasted_iota(jnp.int32, sm.shape, 0)
            sm = jnp.where(rowm < n_extra, sm, -jnp.inf)
            m_new = jnp.maximum(jnp.maximum(m_prev, jnp.max(st, axis=0, keepdims=True)),
                                jnp.max(sm, axis=0, keepdims=True))
            pm_s[...] = jnp.exp2(sm - m_new).astype(BF16)
        else:
            m_new = jnp.maximum(m_prev, mx_bufs[j % 2][...])
        p_bufs[j % 2][...] = jnp.exp2(st - m_new).astype(BF16)
        m_s[...] = m_new
        return jnp.exp2(m_prev - m_new)

    def stage_c(j, alpha, last=False):
        u = jnp.dot(with_ones(vt_ref[j]), p_bufs[j % 2][...], preferred_element_type=F32)
        if last:
            u = u + jnp.dot(with_ones(vmt_ref[...]), pm_s[...], preferred_element_type=F32)
        acc = acc_s[...] + u
        acc_s[...] = acc if alpha is None else alpha * acc

    stage_a(0)

    def full_tick(t):
        alpha = stage_b(t - 1, False)
        if t >= 2:
            stage_c(t - 2, alpha)
        stage_a(t)

    for t in range(1, nk + 2):
        if t <= nk - 1:
            pl.when(t <= n_blocks - 1)(functools.partial(full_tick, t))

        if t <= nk:
            @pl.when(t == n_blocks)
            def _(t=t):
                alpha = stage_b(t - 1, True)
                if t >= 2:
                    stage_c(t - 2, alpha)

        if t >= 2:
            @pl.when(t == n_blocks + 1)
            def _(t=t):
                stage_c(t - 2, None, last=True)

    lam = (jnp.exp(jnp.sum(lq1_ref[...] * lk1_ref[...], axis=-1, keepdims=True))
           - jnp.exp(jnp.sum(lq2_ref[...] * lk2_ref[...], axis=-1, keepdims=True)) + lam_init)
    ot = acc_s[:V_DIM, :] * (1.0 / acc_s[V_DIM:V_DIM + 1, :])
    od = ot[:, :tq] - lam * ot[:, tq:]
    ms = jnp.mean(od * od, axis=0, keepdims=True)
    on = (od * lax.rsqrt(ms + EPS) * subc_ref[...]) * (1.0 - lam_init)
    o_ref[...] = (on.T * sza_ref[...].astype(F32)).astype(BF16)


def _diag_bias(tq):
    k_chunk = lax.broadcasted_iota(jnp.int32, (tq, 2 * tq), 0) >> CHUNK_SHIFT
    q_chunk = (lax.broadcasted_iota(jnp.int32, (tq, 2 * tq), 1) & (tq - 1)) >> CHUNK_SHIFT
    return jnp.where(k_chunk <= q_chunk, 0.0, -jnp.inf).astype(F32)


def _attn_frames(z, vt, km, vmt, subc, lq1, lk1, lq2, lk2, *, bt, t_len, tq, n_extra, lam_init):
    tk = vt.shape[2]
    assert tq == tk, "the diagonal-block mask is shared between query tiles only when tq == tk"
    nq, nk = t_len // tq, t_len // tk
    n = bt * t_len
    hcol = lambda g: (g * D_MODEL) // LANES
    small = lambda c: pl.BlockSpec((1, c), lambda b, h, i: (0, 0))
    return pl.pallas_call(
        functools.partial(_attn_frames_body, tq=tq, tk=tk, nk=nk, n_extra=n_extra,
                          lam_init=lam_init),
        grid=(bt, N_HEADS, nq),
        in_specs=[
            pl.BlockSpec((tq, LANES), lambda b, h, i: (b * nq + i, hcol(G_Q) + h)),
            pl.BlockSpec((t_len, LANES), lambda b, h, i: (b, hcol(G_K) + h)),
            pl.BlockSpec((nk, LANES, tk), lambda b, h, i: (b, h, 0)),
            pl.BlockSpec((LANES, LANES), lambda b, h, i: (0, h)),
            pl.BlockSpec((LANES, LANES), lambda b, h, i: (h, 0)),
            pl.BlockSpec((tk, 2 * tq), lambda b, h, i: (0, 0)),
            pl.BlockSpec((tq, LANES), lambda b, h, i: (b * nq + i, hcol(G_ZA) + h)),
            pl.BlockSpec((V_DIM, 1), lambda b, h, i: (0, 0)),
            small(HEAD_DIM), small(HEAD_DIM), small(HEAD_DIM), small(HEAD_DIM),
        ],
        out_specs=pl.BlockSpec((tq, LANES), lambda b, h, i: (b * nq + i, h)),
        out_shape=_sds((n, D_MODEL), BF16),
        scratch_shapes=[
            pltpu.VMEM((tk, 2 * tq), F32),
            pltpu.VMEM((tk, 2 * tq), F32),
            pltpu.VMEM((tk, 2 * tq), BF16),
            pltpu.VMEM((tk, 2 * tq), BF16),
            pltpu.VMEM((1, 2 * tq), F32),
            pltpu.VMEM((1, 2 * tq), F32),
            pltpu.VMEM((LANES, 2 * tq), BF16),
            pltpu.VMEM((1, 2 * tq), F32),
            pltpu.VMEM((V_DIM + ONES_ROWS, 2 * tq), F32),
        ],
        compiler_params=pltpu.CompilerParams(
            dimension_semantics=("arbitrary", "arbitrary", "arbitrary"),
            vmem_limit_bytes=VMEM_LIMIT),
        name="attn_frames",
    )(z, z, vt, km, vmt, _diag_bias(tq), z, subc, lq1, lk1, lq2, lk2)


def _attn_small_body(*refs, tq, n_own, has_cache, lam_init):
    if has_cache:
        (q_ref, kc_ref, vc_ref, ko_ref, vo_ref, sza_ref, sub_ref, lq1_ref, lk1_ref, lq2_ref,
         lk2_ref, o_ref, m_s, l_s, acc_s) = refs
    else:
        (q_ref, ko_ref, vo_ref, sza_ref, sub_ref, lq1_ref, lk1_ref, lq2_ref,
         lk2_ref, o_ref, m_s, l_s, acc_s) = refs
    q2 = _stack_maps(q_ref[...])
    m_s[...] = jnp.full(m_s.shape, NEG_BIG, F32)
    l_s[...] = jnp.zeros(l_s.shape, F32)
    acc_s[...] = jnp.zeros(acc_s.shape, F32)
    pairs = []
    if has_cache:
        pairs.append((_nt_dot(q2, kc_ref[0, 0].astype(BF16)), vc_ref[0, 0].astype(BF16)))
    so = _nt_dot(q2, ko_ref[...])
    colo = lax.broadcasted_iota(jnp.int32, so.shape, 1)
    so = jnp.where(colo < n_own, so, -jnp.inf)
    pairs.append((so, vo_ref[...]))
    _softmax_update(pairs, m_s, l_s, acc_s)
    o_ref[...] = _attn_finish(acc_s[...], l_s[...], sza_ref[...], sub_ref[...], lq1_ref[...],
                              lk1_ref[...], lq2_ref[...], lk2_ref[...], tq, lam_init)


def _attn_small(z, ko, vo, cache, subw, lq1, lk1, lq2, lk2, *, bt, tq, n_own, layer, lam_init):
    hcol = lambda g: (g * D_MODEL) // LANES
    small = lambda w: pl.BlockSpec((1, w), lambda b, h: (0, 0))
    has_cache = cache is not None
    in_specs = [pl.BlockSpec((tq, LANES), lambda b, h: (b, hcol(G_Q) + h))]
    args = [z]
    if has_cache:
        past = cache[0].shape[2]
        in_specs += [pl.BlockSpec((1, 1, past, LANES), lambda b, h: (layer, b, 0, h))] * 2
        args += list(cache)
    in_specs += [
        pl.BlockSpec((LANES, LANES), lambda b, h: (b, h)),
        pl.BlockSpec((LANES, LANES), lambda b, h: (b, h)),
        pl.BlockSpec((tq, LANES), lambda b, h: (b, hcol(G_ZA) + h)),
        small(V_DIM), small(HEAD_DIM), small(HEAD_DIM), small(HEAD_DIM), small(HEAD_DIM),
    ]
    args += [ko, vo, z, subw, lq1, lk1, lq2, lk2]
    return pl.pallas_call(
        functools.partial(_attn_small_body, tq=tq, n_own=n_own, has_cache=has_cache,
                          lam_init=lam_init),
        grid=(bt, N_HEADS),
        in_specs=in_specs,
        out_specs=pl.BlockSpec((tq, LANES), lambda b, h: (b, h)),
        out_shape=_sds((bt * tq, D_MODEL), BF16),
        scratch_shapes=[
            pltpu.VMEM((2 * tq, 1), F32),
            pltpu.VMEM((2 * tq, 1), F32),
            pltpu.VMEM((2 * tq, V_DIM), F32),
        ],
        compiler_params=pltpu.CompilerParams(
            dimension_semantics=("arbitrary", "arbitrary"), vmem_limit_bytes=VMEM_LIMIT),
        name="attn_small",
    )(*args)


def _merge_body(h_ref, or_ref, oa_ref, gr_ref, ga_ref, wpr_ref, wpa_ref, wo_ref, fnw_ref, out_ref,
                *, final):
    pr = jnp.dot(or_ref[...], wpr_ref[...], preferred_element_type=F32)
    pa = jnp.dot(oa_ref[...], wpa_ref[...], preferred_element_type=F32)
    m = gr_ref[...].astype(F32) * pr + ga_ref[...].astype(F32) * pa
    hn = h_ref[...] + jnp.dot(m.astype(BF16), wo_ref[...], preferred_element_type=F32)
    if final:
        ms = jnp.mean(hn * hn, axis=-1, keepdims=True)
        hn = hn * lax.rsqrt(ms + EPS) * fnw_ref[...]
    out_ref[...] = hn


def _merge(h2d, o_r, o_a, z, wpr, wpa, wo, fnw, *, tm, final):
    n = h2d.shape[0]
    tile = lambda: pl.BlockSpec((tm, D_MODEL), lambda i: (i, 0))
    wspec = lambda: pl.BlockSpec((D_MODEL, D_MODEL), lambda i: (0, 0))
    return pl.pallas_call(
        functools.partial(_merge_body, final=final),
        grid=(n // tm,),
        in_specs=[
            tile(), tile(), tile(),
            pl.BlockSpec((tm, D_MODEL), lambda i: (i, G_GR)),
            pl.BlockSpec((tm, D_MODEL), lambda i: (i, G_GA)),
            wspec(), wspec(), wspec(),
            pl.BlockSpec((1, D_MODEL), lambda i: (0, 0)),
        ],
        out_specs=tile(),
        out_shape=_sds((n, D_MODEL), F32),
        compiler_params=pltpu.CompilerParams(
            dimension_semantics=("arbitrary",), vmem_limit_bytes=VMEM_LIMIT),
        name="merge_out",
    )(h2d, o_r, o_a, z, z, wpr, wpa, wo, fnw)


def _rope_tables(pos):
    half = HEAD_DIM // 2
    inv = 1.0 / (ROPE_THETA ** (jnp.arange(half, dtype=F32) / half))
    ang = pos.astype(F32)[:, None] * inv[None, :]
    cos, sin = jnp.cos(ang), jnp.sin(ang)
    cos_t = jnp.concatenate([cos, cos, cos, cos], axis=1)
    sin_t = jnp.concatenate([-sin, sin, -sin, sin], axis=1)
    return cos_t, sin_t


def _pick_tile(n, pref):
    t = pref
    while n % t:
        t //= 2
    return t


def _pad_rows(x, rows):
    return jnp.pad(x, ((0, rows - x.shape[0]), (0, 0)))


def kernel(x_prompt, x_sample, cache_k, cache_v, state_conv, state_rnn, meta_tokens, norm_w, w_in,
           conv_w, conv_b, w_rg, b_rg, w_ig, b_ig, lru_lambda, lambda_q1, lambda_k1, lambda_q2,
           lambda_k2, subln_w, w_proj_rnn, w_proj_att, w_out, final_norm_w):
    B, SEQ, _ = x_prompt.shape
    DB, S, _ = x_sample.shape
    depth, _, past = cache_k.shape[0], cache_k.shape[1], cache_k.shape[2]
    assert SEQ % 1024 == 0 and S % 16 == 0 and S <= LANES

    w_in_bf = w_in.astype(BF16)
    wg_bf = jnp.concatenate([w_rg, w_ig], axis=-1).astype(BF16)
    wpr_bf, wpa_bf, wo_bf = w_proj_rnn.astype(BF16), w_proj_att.astype(BF16), w_out.astype(BF16)
    cache_k4 = cache_k.reshape(depth, DB, past, D_MODEL)
    cache_v4 = cache_v.reshape(depth, DB, past, D_MODEL)
    fnw = final_norm_w.reshape(1, D_MODEL)

    cos_m, sin_m = _rope_tables(jnp.arange(N_META))
    cos_f, sin_f = _rope_tables(N_META + jnp.arange(SEQ))
    cos_s, sin_s = _rope_tables(N_META + past + jnp.arange(S))
    cos_s, sin_s = jnp.tile(cos_s, (DB, 1)), jnp.tile(sin_s, (DB, 1))

    tm_f = _pick_tile(SEQ, 1024)
    tm_s = _pick_tile(DB * S, 512)
    tt_f = _pick_tile(SEQ, 512)

    def layer_params(l):
        row = lambda a: a[l].reshape(1, -1)
        return dict(
            nw=row(norm_w), w_in=w_in_bf[l], cw=conv_w[l], cb=row(conv_b), wg=wg_bf[l],
            brg=row(b_rg), big=row(b_ig), lam=row(lru_lambda), subw=row(subln_w),
            lq1=row(lambda_q1), lk1=row(lambda_k1), lq2=row(lambda_q2), lk2=row(lambda_k2),
            wpr=wpr_bf[l], wpa=wpa_bf[l], wo=wo_bf[l],
            lam_init=0.8 - 0.6 * math.exp(-0.3 * l))

    def kcols(z):
        return z[:, G_K * D_MODEL:(G_K + 1) * D_MODEL]

    def vcols(z):
        return z[:, G_V * D_MODEL:(G_V + 1) * D_MODEL]

    hm = meta_tokens.astype(F32)
    hf = x_prompt.reshape(B * SEQ, D_MODEL)
    hs = x_sample.reshape(DB * S, D_MODEL)
    zero_prefix = jnp.zeros((1, SUBLANES, D_MODEL), F32)
    zero_h0 = jnp.zeros((1, 1, D_MODEL), F32)

    kp, vp, cp, rp, ks, vs, cs, rs = [], [], [], [], [], [], [], []
    k_all = v_all = None
    for l in range(depth):
        p = layer_params(l)
        final = l == depth - 1
        lam_kw = dict(subw=p["subw"], lq1=p["lq1"], lk1=p["lk1"], lq2=p["lq2"], lk2=p["lk2"])
        rnn_w = (p["cw"], p["cb"], p["wg"], p["brg"], p["big"], p["lam"])

        zm, kvm = _inproj(hm, p["nw"], p["w_in"], cos_m, sin_m, tm=N_META)
        subc = p["subw"].reshape(V_DIM, 1)
        orm, cst_m, hst_m = _rglru(zm, zero_prefix, zero_h0, *rnn_w, bt=1, t_len=N_META, tt=N_META)
        km_pad, vm_pad = _pad_rows(kcols(zm), LANES), _pad_rows(vcols(zm), LANES)
        oam = _attn_small(zm, km_pad, vm_pad, None, **lam_kw, bt=1, tq=N_META, n_own=N_META,
                          layer=l, lam_init=p["lam_init"])
        hm = _merge(hm, orm, oam, zm, p["wpr"], p["wpa"], p["wo"], fnw, tm=N_META, final=False)

        zf, k_all, v_all, vtf = _inproj(hf, p["nw"], p["w_in"], cos_f, sin_f, tm=tm_f, emit_vt=True,
                                        direct=(l, depth, B, SEQ, N_META, k_all, v_all))
        orf, cst_f, hst_f = _rglru(
            zf, jnp.broadcast_to(cst_m, (B, SUBLANES, D_MODEL)),
            jnp.broadcast_to(hst_m, (B, 1, D_MODEL)), *rnn_w, bt=B, t_len=SEQ, tt=tt_f)
        oaf = _attn_frames(zf, vtf, km_pad, vm_pad.T, subc, p["lq1"], p["lk1"], p["lq2"], p["lk2"],
                           bt=B, t_len=SEQ, tq=512, n_extra=N_META, lam_init=p["lam_init"])
        hf = _merge(hf, orf, oaf, zf, p["wpr"], p["wpa"], p["wo"], fnw, tm=tt_f, final=final)

        kp.append(kvm[:, :D_MODEL])
        vp.append(kvm[:, D_MODEL:])
        cp.append(cst_f[:, SUBLANES - (CONV_W - 1):, :])
        rp.append(hst_f[:, 0, :])

        zs, kvs = _inproj(hs, p["nw"], p["w_in"], cos_s, sin_s, tm=tm_s)
        pre_s = jnp.pad(state_conv[l], ((0, 0), (SUBLANES - (CONV_W - 1), 0), (0, 0)))
        ors, cst_s, hst_s = _rglru(zs, pre_s, state_rnn[l][:, None, :], *rnn_w, bt=DB, t_len=S, tt=S)
        pad_own = lambda a: jnp.pad(a.reshape(DB, S, D_MODEL), ((0, 0), (0, LANES - S), (0, 0))
                                    ).reshape(DB * LANES, D_MODEL)
        oas = _attn_small(zs, pad_own(kcols(zs)), pad_own(vcols(zs)), (cache_k4, cache_v4),
                          **lam_kw, bt=DB, tq=S, n_own=S, layer=l, lam_init=p["lam_init"])
        hs = _merge(hs, ors, oas, zs, p["wpr"], p["wpa"], p["wo"], fnw, tm=tm_s, final=final)

        kvs3 = kvs.reshape(DB, S, 2 * D_MODEL)
        ks.append(kvs3[..., :D_MODEL])
        vs.append(kvs3[..., D_MODEL:])
        cs.append(cst_s[:, SUBLANES - (CONV_W - 1):, :])
        rs.append(hst_s[:, 0, :])

    tp = N_META + SEQ
    meta_rows = lambda rows: jnp.broadcast_to(jnp.stack(rows)[:, None], (depth, B, N_META, D_MODEL))
    k_all = k_all.at[:, :, :N_META, :].set(meta_rows(kp))
    v_all = v_all.at[:, :, :N_META, :].set(meta_rows(vp))
    return (
        hf.reshape(B, SEQ, D_MODEL),
        hs.reshape(DB, S, D_MODEL),
        k_all.reshape(depth, B, tp, N_HEADS, 2, HEAD_DIM),
        v_all.reshape(depth, B, tp, N_HEADS, V_DIM),
        jnp.stack(cp),
        jnp.stack(rp),
        jnp.stack(ks).reshape(depth, DB, S, N_HEADS, 2, HEAD_DIM),
        jnp.stack(vs).reshape(depth, DB, S, N_HEADS, V_DIM),
        jnp.stack(cs),
        jnp.stack(rs),
    )
```

```python
import functools
import math

import jax
import jax.numpy as jnp
from jax import lax
from jax.experimental import pallas as pl
from jax.experimental.pallas import tpu as pltpu

F32 = jnp.float32
BF16 = jnp.bfloat16

D_MODEL = 1024
N_HEADS = 8
HEAD_DIM = 64
V_DIM = 128
CHUNK = 64
CHUNK_SHIFT = 6
N_META = 16
CONV_W = 4
LRU_C = 8.0
N_RNN_BLOCKS = 8
RNN_BLOCK = 128
ROPE_THETA = 10000.0
EPS = 1e-6
N_GROUPS = 8
LANES = 128
SUBLANES = 8
G_XR, G_ZR, G_Q, G_K, G_V, G_ZA, G_GR, G_GA = range(8)
Q_SCALE = (HEAD_DIM ** -0.5) * math.log2(math.e)
NEG_BIG = -1e30
ONES_ROWS = 16
KEY_BLOCK = 512


def _sigmoid(x):
    return 0.5 * jnp.tanh(0.5 * x) + 0.5
VMEM_LIMIT = 56 * 1024 * 1024


def _sds(shape, dtype):
    return jax.ShapeDtypeStruct(shape, dtype)


def _nt_dot(a, b):
    return lax.dot_general(a, b, (((1,), (1,)), ((), ())), preferred_element_type=F32)


def _inproj_body(*refs, tm, emit_vt, direct, aliased):
    x_ref, nw_ref, w_ref, cos_ref, sin_ref = refs[:5]
    refs = refs[5 + (2 if aliased else 0):]
    z_ref, refs = refs[0], refs[1:]
    if direct:
        k_ref, v_ref, refs = refs[0], refs[1], refs[2:]
    else:
        kv_ref, refs = refs[0], refs[1:]
    if emit_vt:
        vt_ref, refs = refs[0], refs[1:]
    (xn_ref,) = refs
    j = pl.program_id(1)

    @pl.when(j == 0)
    def _():
        x = x_ref[...]
        ms = jnp.mean(x * x, axis=-1, keepdims=True)
        xn_ref[...] = (x * lax.rsqrt(ms + EPS) * nw_ref[...]).astype(BF16)

    acc = jnp.dot(xn_ref[...], w_ref[...], preferred_element_type=F32)

    def rope_slab(s, first_half):
        rot = jnp.where(first_half, pltpu.roll(s, LANES - 32, 1), pltpu.roll(s, 32, 1))
        return s * cos_ref[...] + rot * sin_ref[...]

    @pl.when(j == G_XR)
    def _():
        z_ref[...] = acc.astype(BF16)

    @pl.when((j == G_ZR) | (j == G_ZA))
    def _():
        z_ref[...] = (acc * _sigmoid(acc)).astype(BF16)

    @pl.when(j == G_Q)
    def _():
        first_half = (lax.broadcasted_iota(jnp.int32, (tm, LANES), 1) & 63) < 32
        for hh in range(N_HEADS):
            sl = slice(hh * LANES, (hh + 1) * LANES)
            z_ref[:, sl] = (rope_slab(acc[:, sl], first_half) * Q_SCALE).astype(BF16)

    @pl.when(j == G_K)
    def _():
        first_half = (lax.broadcasted_iota(jnp.int32, (tm, LANES), 1) & 63) < 32
        for hh in range(N_HEADS):
            sl = slice(hh * LANES, (hh + 1) * LANES)
            r = rope_slab(acc[:, sl], first_half)
            if direct:
                k_ref[0, 0, :, sl] = r
            else:
                kv_ref[:, sl] = r
            z_ref[:, sl] = r.astype(BF16)

    @pl.when(j == G_V)
    def _():
        if direct:
            v_ref[0, 0] = acc
        else:
            kv_ref[...] = acc
        z_ref[...] = acc.astype(BF16)
        if emit_vt:
            for kb in range(tm // KEY_BLOCK):
                vt_ref[kb] = acc[kb * KEY_BLOCK:(kb + 1) * KEY_BLOCK, :].T.astype(BF16)

    @pl.when(j >= G_GR)
    def _():
        z_ref[...] = _sigmoid(acc).astype(BF16)


def _inproj(h2d, nw, w_bf, cos, sin, *, tm, emit_vt=False, direct=None):
    n = h2d.shape[0]
    ntab = cos.shape[0] // tm

    def kv_index(i, j):
        return (i, jnp.where(j > G_K, 1, 0))

    args = [h2d, nw, w_bf, cos, sin]
    in_specs = [
        pl.BlockSpec((tm, D_MODEL), lambda i, j: (i, 0)),
        pl.BlockSpec((1, D_MODEL), lambda i, j: (0, 0)),
        pl.BlockSpec((D_MODEL, D_MODEL), lambda i, j: (0, j)),
        pl.BlockSpec((tm, LANES), lambda i, j: (i % ntab, 0)),
        pl.BlockSpec((tm, LANES), lambda i, j: (i % ntab, 0)),
    ]
    out_specs = [pl.BlockSpec((tm, D_MODEL), lambda i, j: (i, j))]
    out_shape = [_sds((n, N_GROUPS * D_MODEL), BF16)]
    aliases = {}
    if direct is None:
        out_specs.append(pl.BlockSpec((tm, D_MODEL), kv_index))
        out_shape.append(_sds((n, 2 * D_MODEL), F32))
    else:
        layer, depth, bt, t_len, row0, k_buf, v_buf = direct
        nt = t_len // tm
        rows = pl.BlockSpec(
            (pl.Element(1), pl.Element(1), pl.Element(tm), pl.Element(D_MODEL)),
            lambda i, j: (layer, i // nt, pl.multiple_of(row0 + (i % nt) * tm, SUBLANES), 0))
        out_specs += [rows, rows]
        out_shape += [_sds((depth, bt, row0 + t_len, D_MODEL), F32)] * 2
        if k_buf is not None:
            in_specs += [pl.BlockSpec(memory_space=pl.ANY)] * 2
            args += [k_buf, v_buf]
            aliases = {5: 1, 6: 2}
    if emit_vt:
        out_specs.append(pl.BlockSpec((tm // KEY_BLOCK, D_MODEL, KEY_BLOCK), lambda i, j: (i, 0, 0)))
        out_shape.append(_sds((n // KEY_BLOCK, D_MODEL, KEY_BLOCK), BF16))

    return pl.pallas_call(
        functools.partial(_inproj_body, tm=tm, emit_vt=emit_vt, direct=direct is not None,
                          aliased=bool(aliases)),
        grid=(n // tm, N_GROUPS),
        in_specs=in_specs,
        out_specs=out_specs,
        out_shape=out_shape,
        input_output_aliases=aliases,
        scratch_shapes=[pltpu.VMEM((tm, D_MODEL), BF16)],
        compiler_params=pltpu.CompilerParams(
            dimension_semantics=("arbitrary", "arbitrary"), vmem_limit_bytes=VMEM_LIMIT),
        name="inproj",
    )(*args)


def _meta_rows_body(src_ref, k_in, v_in, k_ref, v_ref):
    del k_in, v_in
    k_ref[0, 0] = src_ref[0, :, :D_MODEL]
    v_ref[0, 0] = src_ref[0, :, D_MODEL:]


def _write_meta_rows(k_buf, v_buf, kvm):
    depth, bt = k_buf.shape[:2]
    rows = pl.BlockSpec((pl.Element(1), pl.Element(1), pl.Element(N_META), pl.Element(D_MODEL)),
                        lambda l, b: (l, b, 0, 0))
    return pl.pallas_call(
        _meta_rows_body,
        grid=(depth, bt),
        in_specs=[pl.BlockSpec((1, N_META, 2 * D_MODEL), lambda l, b: (l, 0, 0)),
                  pl.BlockSpec(memory_space=pl.ANY), pl.BlockSpec(memory_space=pl.ANY)],
        out_specs=[rows, rows],
        out_shape=[_sds(k_buf.shape, F32), _sds(v_buf.shape, F32)],
        input_output_aliases={1: 0, 2: 1},
        compiler_params=pltpu.CompilerParams(dimension_semantics=("arbitrary", "arbitrary")),
        name="meta_rows",
    )(kvm, k_buf, v_buf)


def _rglru_body(xr_ref, sz_ref, pre_ref, h0_ref, cw_ref, cb_ref, wg_ref, brg_ref, big_ref, lam_ref,
                o_ref, cst_ref, hst_ref, xbuf, a_s, b_s, h_s, hc, *, tt):
    t = pl.program_id(1)

    @pl.when(t == 0)
    def _():
        xbuf[0:SUBLANES, :] = pre_ref[0]
        hc[...] = jnp.broadcast_to(h0_ref[0], (SUBLANES, D_MODEL))

    xr = xr_ref[...].astype(F32)
    xbuf[SUBLANES:SUBLANES + tt, :] = xr
    cw = cw_ref[...]
    xc = cb_ref[...] + xbuf[SUBLANES - 3:SUBLANES - 3 + tt, :] * cw[0:1, :]
    xc = xc + xbuf[SUBLANES - 2:SUBLANES - 2 + tt, :] * cw[1:2, :]
    xc = xc + xbuf[SUBLANES - 1:SUBLANES - 1 + tt, :] * cw[2:3, :]
    xc = xc + xr * cw[3:4, :]
    xcb = xc.astype(BF16)

    nl = -lam_ref[...]
    softplus = jnp.maximum(nl, 0.0) + jnp.log(1.0 + jnp.exp(-jnp.abs(nl)))
    c = -LRU_C * softplus

    grouped = (tt // SUBLANES, SUBLANES, RNN_BLOCK)
    rowmod = lax.broadcasted_iota(jnp.int32, grouped, 1)
    for n in range(N_RNN_BLOCKS):
        sl = slice(n * RNN_BLOCK, (n + 1) * RNN_BLOCK)
        g = jnp.dot(xcb[:, sl], wg_ref[n], preferred_element_type=F32)
        r = _sigmoid(g[:, :RNN_BLOCK] + brg_ref[:, sl])
        ig = _sigmoid(g[:, RNN_BLOCK:] + big_ref[:, sl])
        log_a = c[:, sl] * r
        a = jnp.exp(log_a)
        y = -jnp.tanh(log_a) * (1.0 + a * a)
        mult = jnp.where(y > 0.0, y * lax.rsqrt(y), 0.0)
        bb = (mult * ig * xc[:, sl]).reshape(grouped)
        a = a.reshape(grouped)
        for d in (1, 2, 4):
            keep = rowmod >= d
            a_p = jnp.where(keep, pltpu.roll(a, d, 1), 1.0)
            b_p = jnp.where(keep, pltpu.roll(bb, d, 1), 0.0)
            bb = bb + a * b_p
            a = a * a_p
        a_s[:, sl] = a.reshape(tt, RNN_BLOCK)
        b_s[:, sl] = bb.reshape(tt, RNN_BLOCK)

    def group(gi, h):
        rows = pl.ds(pl.multiple_of(gi * SUBLANES, SUBLANES), SUBLANES)
        hb = a_s[rows, :] * h + b_s[rows, :]
        h_s[rows, :] = hb
        return jnp.broadcast_to(hb[SUBLANES - 1:SUBLANES, :], (SUBLANES, D_MODEL))

    h = lax.fori_loop(0, tt // SUBLANES, group, hc[...])
    hc[...] = h
    o_ref[...] = (h_s[...] * sz_ref[...].astype(F32)).astype(BF16)
    hst_ref[0] = h[0:1, :]
    tail = xbuf[tt:tt + SUBLANES, :]
    cst_ref[0] = tail
    xbuf[0:SUBLANES, :] = tail


def _rglru(z, prefix, h0, cw, cb, wg, brg, big, lam, *, bt, t_len, tt):
    nt = t_len // tt
    n = bt * t_len
    vec = lambda: pl.BlockSpec((1, D_MODEL), lambda b, t: (0, 0))
    return pl.pallas_call(
        functools.partial(_rglru_body, tt=tt),
        grid=(bt, nt),
        in_specs=[
            pl.BlockSpec((tt, D_MODEL), lambda b, t: (b * nt + t, G_XR)),
            pl.BlockSpec((tt, D_MODEL), lambda b, t: (b * nt + t, G_ZR)),
            pl.BlockSpec((1, SUBLANES, D_MODEL), lambda b, t: (b, 0, 0)),
            pl.BlockSpec((1, 1, D_MODEL), lambda b, t: (b, 0, 0)),
            pl.BlockSpec((CONV_W, D_MODEL), lambda b, t: (0, 0)),
            vec(),
            pl.BlockSpec((N_RNN_BLOCKS, RNN_BLOCK, 2 * RNN_BLOCK), lambda b, t: (0, 0, 0)),
            vec(), vec(), vec(),
        ],
        out_specs=[
            pl.BlockSpec((tt, D_MODEL), lambda b, t: (b * nt + t, 0)),
            pl.BlockSpec((1, SUBLANES, D_MODEL), lambda b, t: (b, 0, 0)),
            pl.BlockSpec((1, 1, D_MODEL), lambda b, t: (b, 0, 0)),
        ],
        out_shape=[_sds((n, D_MODEL), BF16), _sds((bt, SUBLANES, D_MODEL), F32),
                   _sds((bt, 1, D_MODEL), F32)],
        scratch_shapes=[
            pltpu.VMEM((tt + SUBLANES, D_MODEL), F32),
            pltpu.VMEM((tt, D_MODEL), F32),
            pltpu.VMEM((tt, D_MODEL), F32),
            pltpu.VMEM((tt, D_MODEL), F32),
            pltpu.VMEM((SUBLANES, D_MODEL), F32),
        ],
        compiler_params=pltpu.CompilerParams(
            dimension_semantics=("arbitrary", "arbitrary"), vmem_limit_bytes=VMEM_LIMIT),
        name="rglru",
    )(z, z, prefix, h0, cw, cb, wg, brg, big, lam)


def _stack_maps(q):
    lane = lax.broadcasted_iota(jnp.int32, q.shape, 1)
    zero = jnp.zeros_like(q)
    return jnp.concatenate(
        [jnp.where(lane < HEAD_DIM, q, zero), jnp.where(lane >= HEAD_DIM, q, zero)], axis=0)


def _softmax_update(pairs, m_s, l_s, acc_s):
    m_prev = m_s[...]
    m_new = m_prev
    for s, _ in pairs:
        m_new = jnp.maximum(m_new, jnp.max(s, axis=-1, keepdims=True))
    alpha = jnp.exp2(m_prev - m_new)
    l_new = alpha * l_s[...]
    acc = alpha * acc_s[...]
    for s, v in pairs:
        p = jnp.exp2(s - m_new)
        l_new = l_new + jnp.sum(p, axis=-1, keepdims=True)
        acc = acc + jnp.dot(p.astype(BF16), v, preferred_element_type=F32)
    m_s[...] = m_new
    l_s[...] = l_new
    acc_s[...] = acc


def _attn_finish(acc, l, sza, subw, lq1, lk1, lq2, lk2, tq, lam_init):
    o = acc * (1.0 / l)
    lam = (jnp.exp(jnp.sum(lq1 * lk1, axis=-1, keepdims=True))
           - jnp.exp(jnp.sum(lq2 * lk2, axis=-1, keepdims=True)) + lam_init)
    od = o[:tq, :] - lam * o[tq:, :]
    ms = jnp.mean(od * od, axis=-1, keepdims=True)
    on = (od * lax.rsqrt(ms + EPS) * subw) * (1.0 - lam_init)
    return (on * sza.astype(F32)).astype(BF16)


def _attn_frames_body(q_ref, k_ref, vt_ref, km_ref, vmt_ref, bias_ref, sza_ref, subc_ref, lq1_ref,
                      lk1_ref, lq2_ref, lk2_ref, o_ref, s_buf0, s_buf1, p_buf0, p_buf1, mx_buf0, mx_buf1, pm_s,
                      m_s, acc_s, *, tq, tk, nk, n_extra, lam_init):
    i = pl.program_id(2)
    n_blocks = (i * tq) // tk + 1
    s_bufs, p_bufs, mx_bufs = (s_buf0, s_buf1), (p_buf0, p_buf1), (mx_buf0, mx_buf1)

    qt = q_ref[...].astype(F32).T
    rowq = lax.broadcasted_iota(jnp.int32, qt.shape, 0)
    q2t = jnp.concatenate([jnp.where(rowq < HEAD_DIM, qt, 0.0), jnp.where(rowq >= HEAD_DIM, qt, 0.0)],
                          axis=1).astype(BF16)
    m_s[...] = jnp.full(m_s.shape, NEG_BIG, F32)
    acc_s[...] = jnp.zeros(acc_s.shape, F32)

    def with_ones(vt):
        return jnp.concatenate([vt, jnp.ones((ONES_ROWS, vt.shape[1]), BF16)], axis=0)

    def stage_a(j):
        st = jnp.dot(k_ref[j * tk:(j + 1) * tk, :], q2t, preferred_element_type=F32)
        s_bufs[j % 2][...] = st
        mx_bufs[j % 2][...] = jnp.max(st, axis=0, keepdims=True)

    def stage_b(j, diagonal):
        st = s_bufs[j % 2][...]
        m_prev = m_s[...]
        if diagonal:
            st = st + bias_ref[...]
            sm = jnp.dot(km_ref[...], q2t, preferred_element_type=F32)
            rowm = lax.broadcasted_iota(jnp.int32, sm.shape, 0)
            sm = jnp.where(rowm < n_extra, sm, -jnp.inf)
            m_new = jnp.maximum(jnp.maximum(m_prev, jnp.max(st, axis=0, keepdims=True)),
                                jnp.max(sm, axis=0, keepdims=True))
            pm_s[...] = jnp.exp2(sm - m_new).astype(BF16)
        else:
            m_new = jnp.maximum(m_prev, mx_bufs[j % 2][...])
        p_bufs[j % 2][...] = jnp.exp2(st - m_new).astype(BF16)
        m_s[...] = m_new
        return jnp.exp2(m_prev - m_new)

    def stage_c(j, alpha, last=False):
        u = jnp.dot(with_ones(vt_ref[j]), p_bufs[j % 2][...], preferred_element_type=F32)
        if last:
            u = u + jnp.dot(with_ones(vmt_ref[...]), pm_s[...], preferred_element_type=F32)
        acc = acc_s[...] + u
        acc_s[...] = acc if alpha is None else alpha * acc

    stage_a(0)

    def full_tick(t):
        alpha = stage_b(t - 1, False)
        if t >= 2:
            stage_c(t - 2, alpha)
        stage_a(t)

    for t in range(1, nk + 2):
        if t <= nk - 1:
            pl.when(t <= n_blocks - 1)(functools.partial(full_tick, t))

        if t <= nk:
            @pl.when(t == n_blocks)
            def _(t=t):
                alpha = stage_b(t - 1, True)
                if t >= 2:
                    stage_c(t - 2, alpha)

        if t >= 2:
            @pl.when(t == n_blocks + 1)
            def _(t=t):
                stage_c(t - 2, None, last=True)

    lam = (jnp.exp(jnp.sum(lq1_ref[...] * lk1_ref[...], axis=-1, keepdims=True))
           - jnp.exp(jnp.sum(lq2_ref[...] * lk2_ref[...], axis=-1, keepdims=True)) + lam_init)
    ot = acc_s[:V_DIM, :] * (1.0 / acc_s[V_DIM:V_DIM + 1, :])
    od = ot[:, :tq] - lam * ot[:, tq:]
    ms = jnp.mean(od * od, axis=0, keepdims=True)
    on = (od * lax.rsqrt(ms + EPS) * subc_ref[...]) * (1.0 - lam_init)
    o_ref[...] = (on.T * sza_ref[...].astype(F32)).astype(BF16)


def _diag_bias(tq):
    k_chunk = lax.broadcasted_iota(jnp.int32, (tq, 2 * tq), 0) >> CHUNK_SHIFT
    q_chunk = (lax.broadcasted_iota(jnp.int32, (tq, 2 * tq), 1) & (tq - 1)) >> CHUNK_SHIFT
    return jnp.where(k_chunk <= q_chunk, 0.0, -jnp.inf).astype(F32)


def _attn_frames(z, vt, km, vmt, subc, lq1, lk1, lq2, lk2, *, bt, t_len, tq, n_extra, lam_init):
    tk = vt.shape[2]
    assert tq == tk, "the diagonal-block mask is shared between query tiles only when tq == tk"
    nq, nk = t_len // tq, t_len // tk
    n = bt * t_len
    hcol = lambda g: (g * D_MODEL) // LANES
    small = lambda c: pl.BlockSpec((1, c), lambda b, h, i: (0, 0))
    return pl.pallas_call(
        functools.partial(_attn_frames_body, tq=tq, tk=tk, nk=nk, n_extra=n_extra,
                          lam_init=lam_init),
        grid=(bt, N_HEADS, nq),
        in_specs=[
            pl.BlockSpec((tq, LANES), lambda b, h, i: (b * nq + i, hcol(G_Q) + h)),
            pl.BlockSpec((t_len, LANES), lambda b, h, i: (b, hcol(G_K) + h)),
            pl.BlockSpec((nk, LANES, tk), lambda b, h, i: (b, h, 0)),
            pl.BlockSpec((LANES, LANES), lambda b, h, i: (0, h)),
            pl.BlockSpec((LANES, LANES), lambda b, h, i: (h, 0)),
            pl.BlockSpec((tk, 2 * tq), lambda b, h, i: (0, 0)),
            pl.BlockSpec((tq, LANES), lambda b, h, i: (b * nq + i, hcol(G_ZA) + h)),
            pl.BlockSpec((V_DIM, 1), lambda b, h, i: (0, 0)),
            small(HEAD_DIM), small(HEAD_DIM), small(HEAD_DIM), small(HEAD_DIM),
        ],
        out_specs=pl.BlockSpec((tq, LANES), lambda b, h, i: (b * nq + i, h)),
        out_shape=_sds((n, D_MODEL), BF16),
        scratch_shapes=[
            pltpu.VMEM((tk, 2 * tq), F32),
            pltpu.VMEM((tk, 2 * tq), F32),
            pltpu.VMEM((tk, 2 * tq), BF16),
            pltpu.VMEM((tk, 2 * tq), BF16),
            pltpu.VMEM((1, 2 * tq), F32),
            pltpu.VMEM((1, 2 * tq), F32),
            pltpu.VMEM((LANES, 2 * tq), BF16),
            pltpu.VMEM((1, 2 * tq), F32),
            pltpu.VMEM((V_DIM + ONES_ROWS, 2 * tq), F32),
        ],
        compiler_params=pltpu.CompilerParams(
            dimension_semantics=("arbitrary", "arbitrary", "arbitrary"),
            vmem_limit_bytes=VMEM_LIMIT),
        name="attn_frames",
    )(z, z, vt, km, vmt, _diag_bias(tq), z, subc, lq1, lk1, lq2, lk2)


def _attn_flat_body(q_ref, k_ref, vt_ref, km_ref, vmt_ref, bias_ref, sza_ref, subc_ref, lq1_ref,
                    lk1_ref, lq2_ref, lk2_ref, one_ref, o_ref, q2t_s, s_buf0, s_buf1, p_buf0, p_buf1, mx_buf0,
                    mx_buf1, al_buf0, al_buf1, pm_s, m_s, acc_s, *, tq, nq, n_extra, lam_init,
                    ticks_per_block):
    s_bufs, p_bufs = (s_buf0, s_buf1), (p_buf0, p_buf1)
    mx_bufs, al_bufs = (mx_buf0, mx_buf1), (al_buf0, al_buf1)
    pairs = [(i, j) for j in range(nq) for i in range(j, nq)]
    n_pairs = len(pairs)

    for i in range(nq):
        qt = q_ref[i * tq:(i + 1) * tq, :].astype(F32).T
        rowq = lax.broadcasted_iota(jnp.int32, qt.shape, 0)
        q2t_s[i] = jnp.concatenate(
            [jnp.where(rowq < HEAD_DIM, qt, 0.0), jnp.where(rowq >= HEAD_DIM, qt, 0.0)],
            axis=1).astype(BF16)
    m_s[...] = jnp.full(m_s.shape, NEG_BIG, F32)
    acc_s[...] = jnp.zeros(acc_s.shape, F32)
    lam = (jnp.exp(jnp.sum(lq1_ref[...] * lk1_ref[...], axis=-1, keepdims=True))
           - jnp.exp(jnp.sum(lq2_ref[...] * lk2_ref[...], axis=-1, keepdims=True)) + lam_init)

    def with_ones(vt):
        return jnp.concatenate([vt, jnp.ones((ONES_ROWS, vt.shape[1]), BF16)], axis=0)

    def stage_a(p):
        i, j = pairs[p]
        st = jnp.dot(k_ref[j * tq:(j + 1) * tq, :], q2t_s[i], preferred_element_type=F32)
        if i == j:
            st = st + bias_ref[...]
        s_bufs[p % 2][...] = st
        mx_bufs[p % 2][...] = jnp.max(st, axis=0, keepdims=True)

    def stage_b(p):
        i, j = pairs[p]
        m_prev = m_s[i]
        m_new = jnp.maximum(m_prev, mx_bufs[p % 2][...])
        if i == j:
            sm = jnp.dot(km_ref[...], q2t_s[i], preferred_element_type=F32)
            rowm = lax.broadcasted_iota(jnp.int32, sm.shape, 0)
            sm = jnp.where(rowm < n_extra, sm, -jnp.inf)
            m_new = jnp.maximum(m_new, jnp.max(sm, axis=0, keepdims=True))
            pm_s[...] = jnp.exp2(sm - m_new).astype(BF16)
        p_bufs[p % 2][...] = jnp.exp2(s_bufs[p % 2][...] - m_new).astype(BF16)
        m_s[i] = m_new
        al_bufs[p % 2][...] = jnp.exp2(m_prev - m_new)

    def stage_c(p):
        i, j = pairs[p]
        u = jnp.dot(with_ones(vt_ref[j]), p_bufs[p % 2][...], preferred_element_type=F32)
        if i == j:
            u = u + jnp.dot(with_ones(vmt_ref[...]), pm_s[...], preferred_element_type=F32)
        acc = al_bufs[p % 2][...] * acc_s[i] + u
        if i != j:
            acc_s[i] = acc
            return
        ot = acc[:V_DIM, :] * (1.0 / acc[V_DIM:V_DIM + 1, :])
        od = ot[:, :tq] - lam * ot[:, tq:]
        ms = jnp.mean(od * od, axis=0, keepdims=True)
        on = (od * lax.rsqrt(ms + EPS) * subc_ref[...]) * (1.0 - lam_init)
        rows = slice(i * tq, (i + 1) * tq)
        o_ref[rows, :] = (on.T * sza_ref[rows, :].astype(F32)).astype(BF16)

    def run_ticks(ticks):
        for t in ticks:
            if 1 <= t <= n_pairs:
                stage_b(t - 1)
            if t >= 2:
                stage_c(t - 2)
            if t < n_pairs:
                stage_a(t)

    one = one_ref[0]
    all_ticks = list(range(n_pairs + 2))
    for start in range(0, len(all_ticks), ticks_per_block):
        pl.when(one > -start)(functools.partial(run_ticks, all_ticks[start:start + ticks_per_block]))


def _attn_flat(z, vt, km, vmt, subc, lq1, lk1, lq2, lk2, *, bt, t_len, n_extra, lam_init,
               ticks_per_block=1):
    tq = vt.shape[2]
    nq = t_len // tq
    n = bt * t_len
    hcol = lambda g: (g * D_MODEL) // LANES
    small = lambda c: pl.BlockSpec((1, c), lambda b, h: (0, 0))
    return pl.pallas_call(
        functools.partial(_attn_flat_body, tq=tq, nq=nq, n_extra=n_extra, lam_init=lam_init,
                          ticks_per_block=ticks_per_block),
        grid=(bt, N_HEADS),
        in_specs=[
            pl.BlockSpec((t_len, LANES), lambda b, h: (b, hcol(G_Q) + h)),
            pl.BlockSpec((t_len, LANES), lambda b, h: (b, hcol(G_K) + h)),
            pl.BlockSpec((nq, LANES, tq), lambda b, h: (b, h, 0)),
            pl.BlockSpec((LANES, LANES), lambda b, h: (0, h)),
            pl.BlockSpec((LANES, LANES), lambda b, h: (h, 0)),
            pl.BlockSpec((tq, 2 * tq), lambda b, h: (0, 0)),
            pl.BlockSpec((t_len, LANES), lambda b, h: (b, hcol(G_ZA) + h)),
            pl.BlockSpec((V_DIM, 1), lambda b, h: (0, 0)),
            small(HEAD_DIM), small(HEAD_DIM), small(HEAD_DIM), small(HEAD_DIM),
            pl.BlockSpec(memory_space=pltpu.SMEM),
        ],
        out_specs=pl.BlockSpec((t_len, LANES), lambda b, h: (b, h)),
        out_shape=_sds((n, D_MODEL), BF16),
        scratch_shapes=[
            pltpu.VMEM((nq, LANES, 2 * tq), BF16),
            pltpu.VMEM((tq, 2 * tq), F32),
            pltpu.VMEM((tq, 2 * tq), F32),
            pltpu.VMEM((tq, 2 * tq), BF16),
            pltpu.VMEM((tq, 2 * tq), BF16),
            pltpu.VMEM((1, 2 * tq), F32),
            pltpu.VMEM((1, 2 * tq), F32),
            pltpu.VMEM((1, 2 * tq), F32),
            pltpu.VMEM((1, 2 * tq), F32),
            pltpu.VMEM((LANES, 2 * tq), BF16),
            pltpu.VMEM((nq, 1, 2 * tq), F32),
            pltpu.VMEM((nq, V_DIM + ONES_ROWS, 2 * tq), F32),
        ],
        compiler_params=pltpu.CompilerParams(
            dimension_semantics=("arbitrary", "arbitrary"), vmem_limit_bytes=VMEM_LIMIT),
        name="attn_flat",
    )(z, z, vt, km, vmt, _diag_bias(tq), z, subc, lq1, lk1, lq2, lk2, jnp.ones((1,), jnp.int32))


def _attn_small_body(*refs, tq, n_own, has_cache, lam_init):
    if has_cache:
        (q_ref, kc_ref, vc_ref, ko_ref, vo_ref, sza_ref, sub_ref, lq1_ref, lk1_ref, lq2_ref,
         lk2_ref, o_ref, m_s, l_s, acc_s) = refs
    else:
        (q_ref, ko_ref, vo_ref, sza_ref, sub_ref, lq1_ref, lk1_ref, lq2_ref,
         lk2_ref, o_ref, m_s, l_s, acc_s) = refs
    q2 = _stack_maps(q_ref[...])
    m_s[...] = jnp.full(m_s.shape, NEG_BIG, F32)
    l_s[...] = jnp.zeros(l_s.shape, F32)
    acc_s[...] = jnp.zeros(acc_s.shape, F32)
    pairs = []
    if has_cache:
        pairs.append((_nt_dot(q2, kc_ref[0, 0].astype(BF16)), vc_ref[0, 0].astype(BF16)))
    so = _nt_dot(q2, ko_ref[...])
    colo = lax.broadcasted_iota(jnp.int32, so.shape, 1)
    so = jnp.where(colo < n_own, so, -jnp.inf)
    pairs.append((so, vo_ref[...]))
    _softmax_update(pairs, m_s, l_s, acc_s)
    o_ref[...] = _attn_finish(acc_s[...], l_s[...], sza_ref[...], sub_ref[...], lq1_ref[...],
                              lk1_ref[...], lq2_ref[...], lk2_ref[...], tq, lam_init)


def _attn_small(z, ko, vo, cache, subw, lq1, lk1, lq2, lk2, *, bt, tq, n_own, layer, lam_init):
    hcol = lambda g: (g * D_MODEL) // LANES
    small = lambda w: pl.BlockSpec((1, w), lambda b, h: (0, 0))
    has_cache = cache is not None
    in_specs = [pl.BlockSpec((tq, LANES), lambda b, h: (b, hcol(G_Q) + h))]
    args = [z]
    if has_cache:
        past = cache[0].shape[2]
        in_specs += [pl.BlockSpec((1, 1, past, LANES), lambda b, h: (layer, b, 0, h))] * 2
        args += list(cache)
    in_specs += [
        pl.BlockSpec((LANES, LANES), lambda b, h: (b, h)),
        pl.BlockSpec((LANES, LANES), lambda b, h: (b, h)),
        pl.BlockSpec((tq, LANES), lambda b, h: (b, hcol(G_ZA) + h)),
        small(V_DIM), small(HEAD_DIM), small(HEAD_DIM), small(HEAD_DIM), small(HEAD_DIM),
    ]
    args += [ko, vo, z, subw, lq1, lk1, lq2, lk2]
    return pl.pallas_call(
        functools.partial(_attn_small_body, tq=tq, n_own=n_own, has_cache=has_cache,
                          lam_init=lam_init),
        grid=(bt, N_HEADS),
        in_specs=in_specs,
        out_specs=pl.BlockSpec((tq, LANES), lambda b, h: (b, h)),
        out_shape=_sds((bt * tq, D_MODEL), BF16),
        scratch_shapes=[
            pltpu.VMEM((2 * tq, 1), F32),
            pltpu.VMEM((2 * tq, 1), F32),
            pltpu.VMEM((2 * tq, V_DIM), F32),
        ],
        compiler_params=pltpu.CompilerParams(
            dimension_semantics=("arbitrary", "arbitrary"), vmem_limit_bytes=VMEM_LIMIT),
        name="attn_small",
    )(*args)


def _merge_body(h_ref, or_ref, oa_ref, gr_ref, ga_ref, wpr_ref, wpa_ref, wo_ref, fnw_ref, out_ref,
                *, final):
    pr = jnp.dot(or_ref[...], wpr_ref[...], preferred_element_type=F32)
    pa = jnp.dot(oa_ref[...], wpa_ref[...], preferred_element_type=F32)
    m = gr_ref[...].astype(F32) * pr + ga_ref[...].astype(F32) * pa
    hn = h_ref[...] + jnp.dot(m.astype(BF16), wo_ref[...], preferred_element_type=F32)
    if final:
        ms = jnp.mean(hn * hn, axis=-1, keepdims=True)
        hn = hn * lax.rsqrt(ms + EPS) * fnw_ref[...]
    out_ref[...] = hn


def _merge(h2d, o_r, o_a, z, wpr, wpa, wo, fnw, *, tm, final):
    n = h2d.shape[0]
    tile = lambda: pl.BlockSpec((tm, D_MODEL), lambda i: (i, 0))
    wspec = lambda: pl.BlockSpec((D_MODEL, D_MODEL), lambda i: (0, 0))
    return pl.pallas_call(
        functools.partial(_merge_body, final=final),
        grid=(n // tm,),
        in_specs=[
            tile(), tile(), tile(),
            pl.BlockSpec((tm, D_MODEL), lambda i: (i, G_GR)),
            pl.BlockSpec((tm, D_MODEL), lambda i: (i, G_GA)),
            wspec(), wspec(), wspec(),
            pl.BlockSpec((1, D_MODEL), lambda i: (0, 0)),
        ],
        out_specs=tile(),
        out_shape=_sds((n, D_MODEL), F32),
        compiler_params=pltpu.CompilerParams(
            dimension_semantics=("arbitrary",), vmem_limit_bytes=VMEM_LIMIT),
        name="merge_out",
    )(h2d, o_r, o_a, z, z, wpr, wpa, wo, fnw)


def _rope_tables(pos):
    half = HEAD_DIM // 2
    inv = 1.0 / (ROPE_THETA ** (jnp.arange(half, dtype=F32) / half))
    ang = pos.astype(F32)[:, None] * inv[None, :]
    cos, sin = jnp.cos(ang), jnp.sin(ang)
    cos_t = jnp.concatenate([cos, cos, cos, cos], axis=1)
    sin_t = jnp.concatenate([-sin, sin, -sin, sin], axis=1)
    return cos_t, sin_t


def _pick_tile(n, pref):
    t = pref
    while n % t:
        t //= 2
    return t


def _pad_rows(x, rows):
    return jnp.pad(x, ((0, rows - x.shape[0]), (0, 0)))


def kernel(x_prompt, x_sample, cache_k, cache_v, state_conv, state_rnn, meta_tokens, norm_w, w_in,
           conv_w, conv_b, w_rg, b_rg, w_ig, b_ig, lru_lambda, lambda_q1, lambda_k1, lambda_q2,
           lambda_k2, subln_w, w_proj_rnn, w_proj_att, w_out, final_norm_w):
    B, SEQ, _ = x_prompt.shape
    DB, S, _ = x_sample.shape
    depth, _, past = cache_k.shape[0], cache_k.shape[1], cache_k.shape[2]
    assert SEQ % 1024 == 0 and S % 16 == 0 and S <= LANES

    w_in_bf = w_in.astype(BF16)
    wg_bf = jnp.concatenate([w_rg, w_ig], axis=-1).astype(BF16)
    wpr_bf, wpa_bf, wo_bf = w_proj_rnn.astype(BF16), w_proj_att.astype(BF16), w_out.astype(BF16)
    cache_k4 = cache_k.reshape(depth, DB, past, D_MODEL)
    cache_v4 = cache_v.reshape(depth, DB, past, D_MODEL)
    fnw = final_norm_w.reshape(1, D_MODEL)

    cos_m, sin_m = _rope_tables(jnp.arange(N_META))
    cos_f, sin_f = _rope_tables(N_META + jnp.arange(SEQ))
    cos_s, sin_s = _rope_tables(N_META + past + jnp.arange(S))
    cos_s, sin_s = jnp.tile(cos_s, (DB, 1)), jnp.tile(sin_s, (DB, 1))

    tm_f = _pick_tile(SEQ, 1024)
    tm_s = _pick_tile(DB * S, 512)
    tt_f = _pick_tile(SEQ, 512)

    def layer_params(l):
        row = lambda a: a[l].reshape(1, -1)
        return dict(
            nw=row(norm_w), w_in=w_in_bf[l], cw=conv_w[l], cb=row(conv_b), wg=wg_bf[l],
            brg=row(b_rg), big=row(b_ig), lam=row(lru_lambda), subw=row(subln_w),
            lq1=row(lambda_q1), lk1=row(lambda_k1), lq2=row(lambda_q2), lk2=row(lambda_k2),
            wpr=wpr_bf[l], wpa=wpa_bf[l], wo=wo_bf[l],
            lam_init=0.8 - 0.6 * math.exp(-0.3 * l))

    def kcols(z):
        return z[:, G_K * D_MODEL:(G_K + 1) * D_MODEL]

    def vcols(z):
        return z[:, G_V * D_MODEL:(G_V + 1) * D_MODEL]

    hm = meta_tokens.astype(F32)
    hf = x_prompt.reshape(B * SEQ, D_MODEL)
    hs = x_sample.reshape(DB * S, D_MODEL)
    zero_prefix = jnp.zeros((1, SUBLANES, D_MODEL), F32)
    zero_h0 = jnp.zeros((1, 1, D_MODEL), F32)

    kp, vp, cp, rp, ks, vs, cs, rs = [], [], [], [], [], [], [], []
    k_all = v_all = None
    for l in range(depth):
        p = layer_params(l)
        final = l == depth - 1
        lam_kw = dict(subw=p["subw"], lq1=p["lq1"], lk1=p["lk1"], lq2=p["lq2"], lk2=p["lk2"])
        rnn_w = (p["cw"], p["cb"], p["wg"], p["brg"], p["big"], p["lam"])

        zm, kvm = _inproj(hm, p["nw"], p["w_in"], cos_m, sin_m, tm=N_META)
        subc = p["subw"].reshape(V_DIM, 1)
        orm, cst_m, hst_m = _rglru(zm, zero_prefix, zero_h0, *rnn_w, bt=1, t_len=N_META, tt=N_META)
        km_pad, vm_pad = _pad_rows(kcols(zm), LANES), _pad_rows(vcols(zm), LANES)
        oam = _attn_small(zm, km_pad, vm_pad, None, **lam_kw, bt=1, tq=N_META, n_own=N_META,
                          layer=l, lam_init=p["lam_init"])
        hm = _merge(hm, orm, oam, zm, p["wpr"], p["wpa"], p["wo"], fnw, tm=N_META, final=False)

        zf, k_all, v_all, vtf = _inproj(hf, p["nw"], p["w_in"], cos_f, sin_f, tm=tm_f, emit_vt=True,
                                        direct=(l, depth, B, SEQ, N_META, k_all, v_all))
        orf, cst_f, hst_f = _rglru(
            zf, jnp.broadcast_to(cst_m, (B, SUBLANES, D_MODEL)),
            jnp.broadcast_to(hst_m, (B, 1, D_MODEL)), *rnn_w, bt=B, t_len=SEQ, tt=tt_f)
        oaf = _attn_flat(zf, vtf, km_pad, vm_pad.T, subc, p["lq1"], p["lk1"], p["lq2"], p["lk2"],
                         bt=B, t_len=SEQ, n_extra=N_META, lam_init=p["lam_init"])
        hf = _merge(hf, orf, oaf, zf, p["wpr"], p["wpa"], p["wo"], fnw, tm=tt_f, final=final)

        kp.append(kvm)
        cp.append(cst_f[:, SUBLANES - (CONV_W - 1):, :])
        rp.append(hst_f[:, 0, :])

        zs, kvs = _inproj(hs, p["nw"], p["w_in"], cos_s, sin_s, tm=tm_s)
        pre_s = jnp.pad(state_conv[l], ((0, 0), (SUBLANES - (CONV_W - 1), 0), (0, 0)))
        ors, cst_s, hst_s = _rglru(zs, pre_s, state_rnn[l][:, None, :], *rnn_w, bt=DB, t_len=S, tt=S)
        pad_own = lambda a: jnp.pad(a.reshape(DB, S, D_MODEL), ((0, 0), (0, LANES - S), (0, 0))
                                    ).reshape(DB * LANES, D_MODEL)
        oas = _attn_small(zs, pad_own(kcols(zs)), pad_own(vcols(zs)), (cache_k4, cache_v4),
                          **lam_kw, bt=DB, tq=S, n_own=S, layer=l, lam_init=p["lam_init"])
        hs = _merge(hs, ors, oas, zs, p["wpr"], p["wpa"], p["wo"], fnw, tm=tm_s, final=final)

        kvs3 = kvs.reshape(DB, S, 2 * D_MODEL)
        ks.append(kvs3[..., :D_MODEL])
        vs.append(kvs3[..., D_MODEL:])
        cs.append(cst_s[:, SUBLANES - (CONV_W - 1):, :])
        rs.append(hst_s[:, 0, :])

    tp = N_META + SEQ
    k_all, v_all = _write_meta_rows(k_all, v_all, jnp.stack(kp))
    return (
        hf.reshape(B, SEQ, D_MODEL),
        hs.reshape(DB, S, D_MODEL),
        k_all.reshape(depth, B, tp, N_HEADS, 2, HEAD_DIM),
        v_all.reshape(depth, B, tp, N_HEADS, V_DIM),
        jnp.stack(cp),
        jnp.stack(rp),
        jnp.stack(ks).reshape(depth, DB, S, N_HEADS, 2, HEAD_DIM),
        jnp.stack(vs).reshape(depth, DB, S, N_HEADS, V_DIM),
        jnp.stack(cs),
        jnp.stack(rs),
    )
```

```python
import functools
import math

import jax
import jax.numpy as jnp
from jax import lax
from jax.experimental import pallas as pl
from jax.experimental.pallas import tpu as pltpu

F32 = jnp.float32
BF16 = jnp.bfloat16

D_MODEL = 1024
N_HEADS = 8
HEAD_DIM = 64
V_DIM = 128
CHUNK = 64
CHUNK_SHIFT = 6
N_META = 16
CONV_W = 4
LRU_C = 8.0
N_RNN_BLOCKS = 8
RNN_BLOCK = 128
ROPE_THETA = 10000.0
EPS = 1e-6
N_GROUPS = 8
LANES = 128
SUBLANES = 8
G_XR, G_ZR, G_Q, G_K, G_V, G_ZA, G_GR, G_GA = range(8)
Q_SCALE = (HEAD_DIM ** -0.5) * math.log2(math.e)
NEG_BIG = -1e30
ONES_ROWS = 16
KEY_BLOCK = 512


def _sigmoid(x):
    return 0.5 * jnp.tanh(0.5 * x) + 0.5
VMEM_LIMIT = 56 * 1024 * 1024


def _sds(shape, dtype):
    return jax.ShapeDtypeStruct(shape, dtype)


def _nt_dot(a, b):
    return lax.dot_general(a, b, (((1,), (1,)), ((), ())), preferred_element_type=F32)


def _inproj_body(*refs, tm, emit_vt, direct, aliased):
    x_ref, nw_ref, w_ref, cos_ref, sin_ref = refs[:5]
    refs = refs[5 + (2 if aliased else 0):]
    z_ref, refs = refs[0], refs[1:]
    if direct:
        k_ref, v_ref, refs = refs[0], refs[1], refs[2:]
    else:
        kv_ref, refs = refs[0], refs[1:]
    if emit_vt:
        vt_ref, refs = refs[0], refs[1:]
    xn_ref, acc0, acc1 = refs
    accs = (acc0, acc1)
    j = pl.program_id(1)

    def rope_slab(s, first_half):
        rot = jnp.where(first_half, pltpu.roll(s, LANES - 32, 1), pltpu.roll(s, 32, 1))
        return s * cos_ref[...] + rot * sin_ref[...]

    def epilogue(g):
        acc_ref = accs[g % 2]
        slabs = [slice(hh * LANES, (hh + 1) * LANES) for hh in range(D_MODEL // LANES)]
        if g == G_XR:
            z_ref[...] = acc_ref[...].astype(BF16)
        elif g in (G_ZR, G_ZA):
            for sl in slabs:
                a = acc_ref[:, sl]
                z_ref[:, sl] = (a * _sigmoid(a)).astype(BF16)
        elif g in (G_GR, G_GA):
            for sl in slabs:
                z_ref[:, sl] = _sigmoid(acc_ref[:, sl]).astype(BF16)
        elif g in (G_Q, G_K):
            first_half = (lax.broadcasted_iota(jnp.int32, (tm, LANES), 1) & 63) < 32
            for hh in range(N_HEADS):
                sl = slice(hh * LANES, (hh + 1) * LANES)
                r = rope_slab(acc_ref[:, sl], first_half)
                if g == G_Q:
                    z_ref[:, sl] = (r * Q_SCALE).astype(BF16)
                    continue
                if direct:
                    k_ref[0, 0, :, sl] = r
                else:
                    kv_ref[:, sl] = r
                z_ref[:, sl] = r.astype(BF16)
        elif g == G_V:
            acc = acc_ref[...]
            if direct:
                v_ref[0, 0] = acc.reshape(tm, N_HEADS, V_DIM)
            else:
                kv_ref[...] = acc
            z_ref[...] = acc.astype(BF16)
            if emit_vt:
                for kb in range(tm // KEY_BLOCK):
                    vt_ref[kb] = acc[kb * KEY_BLOCK:(kb + 1) * KEY_BLOCK, :].T.astype(BF16)
        else:
            raise ValueError(f"unknown column group {g}")

    def step(g):
        if g == 0:
            x = x_ref[...]
            ms = jnp.mean(x * x, axis=-1, keepdims=True)
            xn_ref[...] = (x * lax.rsqrt(ms + EPS) * nw_ref[...]).astype(BF16)
        else:
            epilogue(g - 1)
        if g < N_GROUPS:
            accs[g % 2][...] = jnp.dot(xn_ref[...], w_ref[...], preferred_element_type=F32)

    for g in range(N_GROUPS + 1):
        pl.when(j == g)(functools.partial(step, g))


def _inproj(h2d, nw, w_bf, cos, sin, *, tm, emit_vt=False, direct=None):
    n = h2d.shape[0]
    ntab = cos.shape[0] // tm

    def kv_index(i, j):
        return (i, jnp.where(j > G_K + 1, 1, 0))

    args = [h2d, nw, w_bf, cos, sin]
    in_specs = [
        pl.BlockSpec((tm, D_MODEL), lambda i, j: (i, 0)),
        pl.BlockSpec((1, D_MODEL), lambda i, j: (0, 0)),
        pl.BlockSpec((D_MODEL, D_MODEL), lambda i, j: (0, jnp.minimum(j, N_GROUPS - 1))),
        pl.BlockSpec((tm, LANES), lambda i, j: (i % ntab, 0)),
        pl.BlockSpec((tm, LANES), lambda i, j: (i % ntab, 0)),
    ]
    out_specs = [pl.BlockSpec((tm, D_MODEL), lambda i, j: (i, jnp.maximum(j - 1, 0)))]
    out_shape = [_sds((n, N_GROUPS * D_MODEL), BF16)]
    aliases = {}
    if direct is None:
        out_specs.append(pl.BlockSpec((tm, D_MODEL), kv_index))
        out_shape.append(_sds((n, 2 * D_MODEL), F32))
    else:
        layer, depth, bt, t_len, row0, k_buf, v_buf = direct
        nt = t_len // tm
        rows = pl.BlockSpec(
            (pl.Element(1), pl.Element(1), pl.Element(tm), pl.Element(D_MODEL)),
            lambda i, j: (layer, i // nt, pl.multiple_of(row0 + (i % nt) * tm, SUBLANES), 0))
        v_rows = pl.BlockSpec(
            (pl.Element(1), pl.Element(1), pl.Element(tm), pl.Element(N_HEADS), pl.Element(V_DIM)),
            lambda i, j: (layer, i // nt, row0 + (i % nt) * tm, 0, 0))
        out_specs += [rows, v_rows]
        out_shape += [_sds((depth, bt, row0 + t_len, D_MODEL), F32),
                      _sds((depth, bt, row0 + t_len, N_HEADS, V_DIM), F32)]
        if k_buf is not None:
            in_specs += [pl.BlockSpec(memory_space=pl.ANY)] * 2
            args += [k_buf, v_buf]
            aliases = {5: 1, 6: 2}
    if emit_vt:
        out_specs.append(pl.BlockSpec((tm // KEY_BLOCK, D_MODEL, KEY_BLOCK), lambda i, j: (i, 0, 0)))
        out_shape.append(_sds((n // KEY_BLOCK, D_MODEL, KEY_BLOCK), BF16))

    return pl.pallas_call(
        functools.partial(_inproj_body, tm=tm, emit_vt=emit_vt, direct=direct is not None,
                          aliased=bool(aliases)),
        grid=(n // tm, N_GROUPS + 1),
        in_specs=in_specs,
        out_specs=out_specs,
        out_shape=out_shape,
        input_output_aliases=aliases,
        scratch_shapes=[pltpu.VMEM((tm, D_MODEL), BF16), pltpu.VMEM((tm, D_MODEL), F32),
                        pltpu.VMEM((tm, D_MODEL), F32)],
        compiler_params=pltpu.CompilerParams(
            dimension_semantics=("arbitrary", "arbitrary"), vmem_limit_bytes=VMEM_LIMIT),
        name="inproj",
    )(*args)


def _meta_rows_body(src_ref, k_in, v_in, k_ref, v_ref):
    del k_in, v_in
    k_ref[0, 0] = src_ref[0, :, :D_MODEL]
    v_ref[0, 0] = src_ref[0, :, D_MODEL:].reshape(N_META, N_HEADS, V_DIM)


def _write_meta_rows(k_buf, v_buf, kvm):
    depth, bt = k_buf.shape[:2]
    rows = pl.BlockSpec((pl.Element(1), pl.Element(1), pl.Element(N_META), pl.Element(D_MODEL)),
                        lambda l, b: (l, b, 0, 0))
    v_rows = pl.BlockSpec((pl.Element(1), pl.Element(1), pl.Element(N_META), pl.Element(N_HEADS),
                           pl.Element(V_DIM)), lambda l, b: (l, b, 0, 0, 0))
    return pl.pallas_call(
        _meta_rows_body,
        grid=(depth, bt),
        in_specs=[pl.BlockSpec((1, N_META, 2 * D_MODEL), lambda l, b: (l, 0, 0)),
                  pl.BlockSpec(memory_space=pl.ANY), pl.BlockSpec(memory_space=pl.ANY)],
        out_specs=[rows, v_rows],
        out_shape=[_sds(k_buf.shape, F32), _sds(v_buf.shape, F32)],
        input_output_aliases={1: 0, 2: 1},
        compiler_params=pltpu.CompilerParams(dimension_semantics=("arbitrary", "arbitrary")),
        name="meta_rows",
    )(kvm, k_buf, v_buf)


def _rglru_body(xr_ref, sz_ref, pre_ref, h0_ref, cw_ref, cb_ref, wg_ref, brg_ref, big_ref, lam_ref,
                o_ref, cst_ref, hst_ref, xbuf, a_s, b_s, h_s, hc, *, tt):
    t = pl.program_id(1)

    @pl.when(t == 0)
    def _():
        xbuf[0:SUBLANES, :] = pre_ref[0]
        hc[...] = jnp.broadcast_to(h0_ref[0], (SUBLANES, D_MODEL))

    xr = xr_ref[...].astype(F32)
    xbuf[SUBLANES:SUBLANES + tt, :] = xr
    cw = cw_ref[...]
    xc = cb_ref[...] + xbuf[SUBLANES - 3:SUBLANES - 3 + tt, :] * cw[0:1, :]
    xc = xc + xbuf[SUBLANES - 2:SUBLANES - 2 + tt, :] * cw[1:2, :]
    xc = xc + xbuf[SUBLANES - 1:SUBLANES - 1 + tt, :] * cw[2:3, :]
    xc = xc + xr * cw[3:4, :]
    xcb = xc.astype(BF16)

    nl = -lam_ref[...]
    softplus = jnp.maximum(nl, 0.0) + jnp.log(1.0 + jnp.exp(-jnp.abs(nl)))
    c = -LRU_C * softplus

    grouped = (tt // SUBLANES, SUBLANES, RNN_BLOCK)
    rowmod = lax.broadcasted_iota(jnp.int32, grouped, 1)
    for n in range(N_RNN_BLOCKS):
        sl = slice(n * RNN_BLOCK, (n + 1) * RNN_BLOCK)
        g = jnp.dot(xcb[:, sl], wg_ref[n], preferred_element_type=F32)
        r = _sigmoid(g[:, :RNN_BLOCK] + brg_ref[:, sl])
        ig = _sigmoid(g[:, RNN_BLOCK:] + big_ref[:, sl])
        log_a = c[:, sl] * r
        a = jnp.exp(log_a)
        y = -jnp.tanh(log_a) * (1.0 + a * a)
        mult = jnp.where(y > 0.0, y * lax.rsqrt(y), 0.0)
        bb = (mult * ig * xc[:, sl]).reshape(grouped)
        a = a.reshape(grouped)
        for d in (1, 2, 4):
            keep = rowmod >= d
            a_p = jnp.where(keep, pltpu.roll(a, d, 1), 1.0)
            b_p = jnp.where(keep, pltpu.roll(bb, d, 1), 0.0)
            bb = bb + a * b_p
            a = a * a_p
        a_s[:, sl] = a.reshape(tt, RNN_BLOCK)
        b_s[:, sl] = bb.reshape(tt, RNN_BLOCK)

    def group(gi, h):
        rows = pl.ds(pl.multiple_of(gi * SUBLANES, SUBLANES), SUBLANES)
        hb = a_s[rows, :] * h + b_s[rows, :]
        h_s[rows, :] = hb
        return jnp.broadcast_to(hb[SUBLANES - 1:SUBLANES, :], (SUBLANES, D_MODEL))

    h = lax.fori_loop(0, tt // SUBLANES, group, hc[...])
    hc[...] = h
    o_ref[...] = (h_s[...] * sz_ref[...].astype(F32)).astype(BF16)
    hst_ref[0] = h[0:1, :]
    tail = xbuf[tt:tt + SUBLANES, :]
    cst_ref[0] = tail
    xbuf[0:SUBLANES, :] = tail


def _rglru(z, prefix, h0, cw, cb, wg, brg, big, lam, *, bt, t_len, tt):
    nt = t_len // tt
    n = bt * t_len
    vec = lambda: pl.BlockSpec((1, D_MODEL), lambda b, t: (0, 0))
    return pl.pallas_call(
        functools.partial(_rglru_body, tt=tt),
        grid=(bt, nt),
        in_specs=[
            pl.BlockSpec((tt, D_MODEL), lambda b, t: (b * nt + t, G_XR)),
            pl.BlockSpec((tt, D_MODEL), lambda b, t: (b * nt + t, G_ZR)),
            pl.BlockSpec((1, SUBLANES, D_MODEL), lambda b, t: (b, 0, 0)),
            pl.BlockSpec((1, 1, D_MODEL), lambda b, t: (b, 0, 0)),
            pl.BlockSpec((CONV_W, D_MODEL), lambda b, t: (0, 0)),
            vec(),
            pl.BlockSpec((N_RNN_BLOCKS, RNN_BLOCK, 2 * RNN_BLOCK), lambda b, t: (0, 0, 0)),
            vec(), vec(), vec(),
        ],
        out_specs=[
            pl.BlockSpec((tt, D_MODEL), lambda b, t: (b * nt + t, 0)),
            pl.BlockSpec((1, SUBLANES, D_MODEL), lambda b, t: (b, 0, 0)),
            pl.BlockSpec((1, 1, D_MODEL), lambda b, t: (b, 0, 0)),
        ],
        out_shape=[_sds((n, D_MODEL), BF16), _sds((bt, SUBLANES, D_MODEL), F32),
                   _sds((bt, 1, D_MODEL), F32)],
        scratch_shapes=[
            pltpu.VMEM((tt + SUBLANES, D_MODEL), F32),
            pltpu.VMEM((tt, D_MODEL), F32),
            pltpu.VMEM((tt, D_MODEL), F32),
            pltpu.VMEM((tt, D_MODEL), F32),
            pltpu.VMEM((SUBLANES, D_MODEL), F32),
        ],
        compiler_params=pltpu.CompilerParams(
            dimension_semantics=("arbitrary", "arbitrary"), vmem_limit_bytes=VMEM_LIMIT),
        name="rglru",
    )(z, z, prefix, h0, cw, cb, wg, brg, big, lam)


def _stack_maps(q):
    lane = lax.broadcasted_iota(jnp.int32, q.shape, 1)
    zero = jnp.zeros_like(q)
    return jnp.concatenate(
        [jnp.where(lane < HEAD_DIM, q, zero), jnp.where(lane >= HEAD_DIM, q, zero)], axis=0)


def _softmax_update(pairs, m_s, l_s, acc_s):
    m_prev = m_s[...]
    m_new = m_prev
    for s, _ in pairs:
        m_new = jnp.maximum(m_new, jnp.max(s, axis=-1, keepdims=True))
    alpha = jnp.exp2(m_prev - m_new)
    l_new = alpha * l_s[...]
    acc = alpha * acc_s[...]
    for s, v in pairs:
        p = jnp.exp2(s - m_new)
        l_new = l_new + jnp.sum(p, axis=-1, keepdims=True)
        acc = acc + jnp.dot(p.astype(BF16), v, preferred_element_type=F32)
    m_s[...] = m_new
    l_s[...] = l_new
    acc_s[...] = acc


def _attn_finish(acc, l, sza, subw, lq1, lk1, lq2, lk2, tq, lam_init):
    o = acc * (1.0 / l)
    lam = (jnp.exp(jnp.sum(lq1 * lk1, axis=-1, keepdims=True))
           - jnp.exp(jnp.sum(lq2 * lk2, axis=-1, keepdims=True)) + lam_init)
    od = o[:tq, :] - lam * o[tq:, :]
    ms = jnp.mean(od * od, axis=-1, keepdims=True)
    on = (od * lax.rsqrt(ms + EPS) * subw) * (1.0 - lam_init)
    return (on * sza.astype(F32)).astype(BF16)


def _diag_bias(tq):
    k_chunk = lax.broadcasted_iota(jnp.int32, (tq, 2 * tq), 0) >> CHUNK_SHIFT
    q_chunk = (lax.broadcasted_iota(jnp.int32, (tq, 2 * tq), 1) & (tq - 1)) >> CHUNK_SHIFT
    return jnp.where(k_chunk <= q_chunk, 0.0, -jnp.inf).astype(F32)


def _attn_flat_body(q_ref, k_ref, vt_ref, km_ref, vmt_ref, bias_ref, sza_ref, subc_ref, lq1_ref,
                    lk1_ref, lq2_ref, lk2_ref, one_ref, o_ref, q2t_s, s_buf0, s_buf1, p_buf0, p_buf1, mx_buf0,
                    mx_buf1, al_buf0, al_buf1, pm_s, m_s, acc_s, *, tq, nq, n_extra, lam_init,
                    ticks_per_block):
    s_bufs, p_bufs = (s_buf0, s_buf1), (p_buf0, p_buf1)
    mx_bufs, al_bufs = (mx_buf0, mx_buf1), (al_buf0, al_buf1)
    pairs = [(i, j) for j in range(nq) for i in range(j, nq)]
    n_pairs = len(pairs)

    for i in range(nq):
        qt = q_ref[i * tq:(i + 1) * tq, :].astype(F32).T
        rowq = lax.broadcasted_iota(jnp.int32, qt.shape, 0)
        q2t_s[i] = jnp.concatenate(
            [jnp.where(rowq < HEAD_DIM, qt, 0.0), jnp.where(rowq >= HEAD_DIM, qt, 0.0)],
            axis=1).astype(BF16)
    m_s[...] = jnp.full(m_s.shape, NEG_BIG, F32)
    acc_s[...] = jnp.zeros(acc_s.shape, F32)
    lam = (jnp.exp(jnp.sum(lq1_ref[...] * lk1_ref[...], axis=-1, keepdims=True))
           - jnp.exp(jnp.sum(lq2_ref[...] * lk2_ref[...], axis=-1, keepdims=True)) + lam_init)

    def with_ones(vt):
        return jnp.concatenate([vt, jnp.ones((ONES_ROWS, vt.shape[1]), BF16)], axis=0)

    def stage_a(p):
        i, j = pairs[p]
        st = jnp.dot(k_ref[j * tq:(j + 1) * tq, :], q2t_s[i], preferred_element_type=F32)
        if i == j:
            st = st + bias_ref[...]
        s_bufs[p % 2][...] = st
        mx_bufs[p % 2][...] = jnp.max(st, axis=0, keepdims=True)

    def stage_b(p):
        i, j = pairs[p]
        m_prev = m_s[i]
        m_new = jnp.maximum(m_prev, mx_bufs[p % 2][...])
        if i == j:
            sm = jnp.dot(km_ref[...], q2t_s[i], preferred_element_type=F32)
            rowm = lax.broadcasted_iota(jnp.int32, sm.shape, 0)
            sm = jnp.where(rowm < n_extra, sm, -jnp.inf)
            m_new = jnp.maximum(m_new, jnp.max(sm, axis=0, keepdims=True))
            pm_s[...] = jnp.exp2(sm - m_new).astype(BF16)
        p_bufs[p % 2][...] = jnp.exp2(s_bufs[p % 2][...] - m_new).astype(BF16)
        m_s[i] = m_new
        al_bufs[p % 2][...] = jnp.exp2(m_prev - m_new)

    def stage_c(p):
        i, j = pairs[p]
        u = jnp.dot(with_ones(vt_ref[j]), p_bufs[p % 2][...], preferred_element_type=F32)
        if i == j:
            u = u + jnp.dot(with_ones(vmt_ref[...]), pm_s[...], preferred_element_type=F32)
        acc = al_bufs[p % 2][...] * acc_s[i] + u
        if i != j:
            acc_s[i] = acc
            return
        ot = acc[:V_DIM, :] * (1.0 / acc[V_DIM:V_DIM + 1, :])
        od = ot[:, :tq] - lam * ot[:, tq:]
        ms = jnp.mean(od * od, axis=0, keepdims=True)
        on = (od * lax.rsqrt(ms + EPS) * subc_ref[...]) * (1.0 - lam_init)
        rows = slice(i * tq, (i + 1) * tq)
        o_ref[rows, :] = (on.T * sza_ref[rows, :].astype(F32)).astype(BF16)

    def run_ticks(ticks):
        for t in ticks:
            if 1 <= t <= n_pairs:
                stage_b(t - 1)
            if t >= 2:
                stage_c(t - 2)
            if t < n_pairs:
                stage_a(t)

    one = one_ref[0]
    all_ticks = list(range(n_pairs + 2))
    for start in range(0, len(all_ticks), ticks_per_block):
        pl.when(one > -start)(functools.partial(run_ticks, all_ticks[start:start + ticks_per_block]))


def _attn_flat(z, vt, km, vmt, subc, lq1, lk1, lq2, lk2, *, bt, t_len, n_extra, lam_init,
               ticks_per_block=1):
    tq = vt.shape[2]
    nq = t_len // tq
    n = bt * t_len
    hcol = lambda g: (g * D_MODEL) // LANES
    small = lambda c: pl.BlockSpec((1, c), lambda b, h: (0, 0))
    return pl.pallas_call(
        functools.partial(_attn_flat_body, tq=tq, nq=nq, n_extra=n_extra, lam_init=lam_init,
                          ticks_per_block=ticks_per_block),
        grid=(bt, N_HEADS),
        in_specs=[
            pl.BlockSpec((t_len, LANES), lambda b, h: (b, hcol(G_Q) + h)),
            pl.BlockSpec((t_len, LANES), lambda b, h: (b, hcol(G_K) + h)),
            pl.BlockSpec((nq, LANES, tq), lambda b, h: (b, h, 0)),
            pl.BlockSpec((LANES, LANES), lambda b, h: (0, h)),
            pl.BlockSpec((LANES, LANES), lambda b, h: (h, 0)),
            pl.BlockSpec((tq, 2 * tq), lambda b, h: (0, 0)),
            pl.BlockSpec((t_len, LANES), lambda b, h: (b, hcol(G_ZA) + h)),
            pl.BlockSpec((V_DIM, 1), lambda b, h: (0, 0)),
            small(HEAD_DIM), small(HEAD_DIM), small(HEAD_DIM), small(HEAD_DIM),
            pl.BlockSpec(memory_space=pltpu.SMEM),
        ],
        out_specs=pl.BlockSpec((t_len, LANES), lambda b, h: (b, h)),
        out_shape=_sds((n, D_MODEL), BF16),
        scratch_shapes=[
            pltpu.VMEM((nq, LANES, 2 * tq), BF16),
            pltpu.VMEM((tq, 2 * tq), F32),
            pltpu.VMEM((tq, 2 * tq), F32),
            pltpu.VMEM((tq, 2 * tq), BF16),
            pltpu.VMEM((tq, 2 * tq), BF16),
            pltpu.VMEM((1, 2 * tq), F32),
            pltpu.VMEM((1, 2 * tq), F32),
            pltpu.VMEM((1, 2 * tq), F32),
            pltpu.VMEM((1, 2 * tq), F32),
            pltpu.VMEM((LANES, 2 * tq), BF16),
            pltpu.VMEM((nq, 1, 2 * tq), F32),
            pltpu.VMEM((nq, V_DIM + ONES_ROWS, 2 * tq), F32),
        ],
        compiler_params=pltpu.CompilerParams(
            dimension_semantics=("arbitrary", "arbitrary"), vmem_limit_bytes=VMEM_LIMIT),
        name="attn_flat",
    )(z, z, vt, km, vmt, _diag_bias(tq), z, subc, lq1, lk1, lq2, lk2, jnp.ones((1,), jnp.int32))


def _attn_small_body(*refs, tq, n_own, has_cache, lam_init):
    if has_cache:
        (q_ref, kc_ref, vc_ref, ko_ref, vo_ref, sza_ref, sub_ref, lq1_ref, lk1_ref, lq2_ref,
         lk2_ref, o_ref, m_s, l_s, acc_s) = refs
    else:
        (q_ref, ko_ref, vo_ref, sza_ref, sub_ref, lq1_ref, lk1_ref, lq2_ref,
         lk2_ref, o_ref, m_s, l_s, acc_s) = refs
    q2 = _stack_maps(q_ref[...])
    m_s[...] = jnp.full(m_s.shape, NEG_BIG, F32)
    l_s[...] = jnp.zeros(l_s.shape, F32)
    acc_s[...] = jnp.zeros(acc_s.shape, F32)
    pairs = []
    if has_cache:
        pairs.append((_nt_dot(q2, kc_ref[0, 0].astype(BF16)), vc_ref[0, 0].astype(BF16)))
    so = _nt_dot(q2, ko_ref[...])
    colo = lax.broadcasted_iota(jnp.int32, so.shape, 1)
    so = jnp.where(colo < n_own, so, -jnp.inf)
    pairs.append((so, vo_ref[...]))
    _softmax_update(pairs, m_s, l_s, acc_s)
    o_ref[...] = _attn_finish(acc_s[...], l_s[...], sza_ref[...], sub_ref[...], lq1_ref[...],
                              lk1_ref[...], lq2_ref[...], lk2_ref[...], tq, lam_init)


def _attn_small(z, ko, vo, cache, subw, lq1, lk1, lq2, lk2, *, bt, tq, n_own, layer, lam_init):
    hcol = lambda g: (g * D_MODEL) // LANES
    small = lambda w: pl.BlockSpec((1, w), lambda b, h: (0, 0))
    has_cache = cache is not None
    in_specs = [pl.BlockSpec((tq, LANES), lambda b, h: (b, hcol(G_Q) + h))]
    args = [z]
    if has_cache:
        past = cache[0].shape[2]
        in_specs += [pl.BlockSpec((1, 1, past, LANES), lambda b, h: (layer, b, 0, h))] * 2
        args += list(cache)
    in_specs += [
        pl.BlockSpec((LANES, LANES), lambda b, h: (b, h)),
        pl.BlockSpec((LANES, LANES), lambda b, h: (b, h)),
        pl.BlockSpec((tq, LANES), lambda b, h: (b, hcol(G_ZA) + h)),
        small(V_DIM), small(HEAD_DIM), small(HEAD_DIM), small(HEAD_DIM), small(HEAD_DIM),
    ]
    args += [ko, vo, z, subw, lq1, lk1, lq2, lk2]
    return pl.pallas_call(
        functools.partial(_attn_small_body, tq=tq, n_own=n_own, has_cache=has_cache,
                          lam_init=lam_init),
        grid=(bt, N_HEADS),
        in_specs=in_specs,
        out_specs=pl.BlockSpec((tq, LANES), lambda b, h: (b, h)),
        out_shape=_sds((bt * tq, D_MODEL), BF16),
        scratch_shapes=[
            pltpu.VMEM((2 * tq, 1), F32),
            pltpu.VMEM((2 * tq, 1), F32),
            pltpu.VMEM((2 * tq, V_DIM), F32),
        ],
        compiler_params=pltpu.CompilerParams(
            dimension_semantics=("arbitrary", "arbitrary"), vmem_limit_bytes=VMEM_LIMIT),
        name="attn_small",
    )(*args)


def _merge_body(h_ref, or_ref, oa_ref, gr_ref, ga_ref, wpr_ref, wpa_ref, wo_ref, fnw_ref, out_ref,
                *, final):
    pr = jnp.dot(or_ref[...], wpr_ref[...], preferred_element_type=F32)
    pa = jnp.dot(oa_ref[...], wpa_ref[...], preferred_element_type=F32)
    m = gr_ref[...].astype(F32) * pr + ga_ref[...].astype(F32) * pa
    hn = h_ref[...] + jnp.dot(m.astype(BF16), wo_ref[...], preferred_element_type=F32)
    if final:
        ms = jnp.mean(hn * hn, axis=-1, keepdims=True)
        hn = hn * lax.rsqrt(ms + EPS) * fnw_ref[...]
    out_ref[...] = hn


def _merge(h2d, o_r, o_a, z, wpr, wpa, wo, fnw, *, tm, final):
    n = h2d.shape[0]
    tile = lambda: pl.BlockSpec((tm, D_MODEL), lambda i: (i, 0))
    wspec = lambda: pl.BlockSpec((D_MODEL, D_MODEL), lambda i: (0, 0))
    return pl.pallas_call(
        functools.partial(_merge_body, final=final),
        grid=(n // tm,),
        in_specs=[
            tile(), tile(), tile(),
            pl.BlockSpec((tm, D_MODEL), lambda i: (i, G_GR)),
            pl.BlockSpec((tm, D_MODEL), lambda i: (i, G_GA)),
            wspec(), wspec(), wspec(),
            pl.BlockSpec((1, D_MODEL), lambda i: (0, 0)),
        ],
        out_specs=tile(),
        out_shape=_sds((n, D_MODEL), F32),
        compiler_params=pltpu.CompilerParams(
            dimension_semantics=("arbitrary",), vmem_limit_bytes=VMEM_LIMIT),
        name="merge_out",
    )(h2d, o_r, o_a, z, z, wpr, wpa, wo, fnw)


def _rope_tables(pos):
    half = HEAD_DIM // 2
    inv = 1.0 / (ROPE_THETA ** (jnp.arange(half, dtype=F32) / half))
    ang = pos.astype(F32)[:, None] * inv[None, :]
    cos, sin = jnp.cos(ang), jnp.sin(ang)
    cos_t = jnp.concatenate([cos, cos, cos, cos], axis=1)
    sin_t = jnp.concatenate([-sin, sin, -sin, sin], axis=1)
    return cos_t, sin_t


def _pick_tile(n, pref):
    t = pref
    while n % t:
        t //= 2
    return t


def _pad_rows(x, rows):
    return jnp.pad(x, ((0, rows - x.shape[0]), (0, 0)))


def kernel(x_prompt, x_sample, cache_k, cache_v, state_conv, state_rnn, meta_tokens, norm_w, w_in,
           conv_w, conv_b, w_rg, b_rg, w_ig, b_ig, lru_lambda, lambda_q1, lambda_k1, lambda_q2,
           lambda_k2, subln_w, w_proj_rnn, w_proj_att, w_out, final_norm_w):
    B, SEQ, _ = x_prompt.shape
    DB, S, _ = x_sample.shape
    depth, _, past = cache_k.shape[0], cache_k.shape[1], cache_k.shape[2]
    assert SEQ % 1024 == 0 and S % 16 == 0 and S <= LANES

    w_in_bf = w_in.astype(BF16)
    wg_bf = jnp.concatenate([w_rg, w_ig], axis=-1).astype(BF16)
    wpr_bf, wpa_bf, wo_bf = w_proj_rnn.astype(BF16), w_proj_att.astype(BF16), w_out.astype(BF16)
    cache_k4 = cache_k.reshape(depth, DB, past, D_MODEL)
    cache_v4 = cache_v.reshape(depth, DB, past, D_MODEL)
    fnw = final_norm_w.reshape(1, D_MODEL)

    cos_m, sin_m = _rope_tables(jnp.arange(N_META))
    cos_f, sin_f = _rope_tables(N_META + jnp.arange(SEQ))
    cos_s, sin_s = _rope_tables(N_META + past + jnp.arange(S))
    cos_s, sin_s = jnp.tile(cos_s, (DB, 1)), jnp.tile(sin_s, (DB, 1))

    tm_f = _pick_tile(SEQ, 1024)
    tm_s = _pick_tile(DB * S, 512)
    tt_f = _pick_tile(SEQ, 512)

    def layer_params(l):
        row = lambda a: a[l].reshape(1, -1)
        return dict(
            nw=row(norm_w), w_in=w_in_bf[l], cw=conv_w[l], cb=row(conv_b), wg=wg_bf[l],
            brg=row(b_rg), big=row(b_ig), lam=row(lru_lambda), subw=row(subln_w),
            lq1=row(lambda_q1), lk1=row(lambda_k1), lq2=row(lambda_q2), lk2=row(lambda_k2),
            wpr=wpr_bf[l], wpa=wpa_bf[l], wo=wo_bf[l],
            lam_init=0.8 - 0.6 * math.exp(-0.3 * l))

    def kcols(z):
        return z[:, G_K * D_MODEL:(G_K + 1) * D_MODEL]

    def vcols(z):
        return z[:, G_V * D_MODEL:(G_V + 1) * D_MODEL]

    hm = meta_tokens.astype(F32)
    hf = x_prompt.reshape(B * SEQ, D_MODEL)
    hs = x_sample.reshape(DB * S, D_MODEL)
    zero_prefix = jnp.zeros((1, SUBLANES, D_MODEL), F32)
    zero_h0 = jnp.zeros((1, 1, D_MODEL), F32)

    kp, cp, rp, ks, vs, cs, rs = [], [], [], [], [], [], []
    k_all = v_all = None
    for l in range(depth):
        p = layer_params(l)
        final = l == depth - 1
        lam_kw = dict(subw=p["subw"], lq1=p["lq1"], lk1=p["lk1"], lq2=p["lq2"], lk2=p["lk2"])
        rnn_w = (p["cw"], p["cb"], p["wg"], p["brg"], p["big"], p["lam"])

        zm, kvm = _inproj(hm, p["nw"], p["w_in"], cos_m, sin_m, tm=N_META)
        subc = p["subw"].reshape(V_DIM, 1)
        orm, cst_m, hst_m = _rglru(zm, zero_prefix, zero_h0, *rnn_w, bt=1, t_len=N_META, tt=N_META)
        km_pad, vm_pad = _pad_rows(kcols(zm), LANES), _pad_rows(vcols(zm), LANES)
        oam = _attn_small(zm, km_pad, vm_pad, None, **lam_kw, bt=1, tq=N_META, n_own=N_META,
                          layer=l, lam_init=p["lam_init"])
        hm = _merge(hm, orm, oam, zm, p["wpr"], p["wpa"], p["wo"], fnw, tm=N_META, final=False)

        zf, k_all, v_all, vtf = _inproj(hf, p["nw"], p["w_in"], cos_f, sin_f, tm=tm_f, emit_vt=True,
                                        direct=(l, depth, B, SEQ, N_META, k_all, v_all))
        orf, cst_f, hst_f = _rglru(
            zf, jnp.broadcast_to(cst_m, (B, SUBLANES, D_MODEL)),
            jnp.broadcast_to(hst_m, (B, 1, D_MODEL)), *rnn_w, bt=B, t_len=SEQ, tt=tt_f)
        oaf = _attn_flat(zf, vtf, km_pad, vm_pad.T, subc, p["lq1"], p["lk1"], p["lq2"], p["lk2"],
                         bt=B, t_len=SEQ, n_extra=N_META, lam_init=p["lam_init"])
        hf = _merge(hf, orf, oaf, zf, p["wpr"], p["wpa"], p["wo"], fnw, tm=tt_f, final=final)

        kp.append(kvm)
        cp.append(cst_f[:, SUBLANES - (CONV_W - 1):, :])
        rp.append(hst_f[:, 0, :])

        zs, kvs = _inproj(hs, p["nw"], p["w_in"], cos_s, sin_s, tm=tm_s)
        pre_s = jnp.pad(state_conv[l], ((0, 0), (SUBLANES - (CONV_W - 1), 0), (0, 0)))
        ors, cst_s, hst_s = _rglru(zs, pre_s, state_rnn[l][:, None, :], *rnn_w, bt=DB, t_len=S, tt=S)
        pad_own = lambda a: jnp.pad(a.reshape(DB, S, D_MODEL), ((0, 0), (0, LANES - S), (0, 0))
                                    ).reshape(DB * LANES, D_MODEL)
        oas = _attn_small(zs, pad_own(kcols(zs)), pad_own(vcols(zs)), (cache_k4, cache_v4),
                          **lam_kw, bt=DB, tq=S, n_own=S, layer=l, lam_init=p["lam_init"])
        hs = _merge(hs, ors, oas, zs, p["wpr"], p["wpa"], p["wo"], fnw, tm=tm_s, final=final)

        kvs3 = kvs.reshape(DB, S, 2 * D_MODEL)
        ks.append(kvs3[..., :D_MODEL])
        vs.append(kvs3[..., D_MODEL:])
        cs.append(cst_s[:, SUBLANES - (CONV_W - 1):, :])
        rs.append(hst_s[:, 0, :])

    tp = N_META + SEQ
    k_all, v_all = _write_meta_rows(k_all, v_all, jnp.stack(kp))
    return (
        hf.reshape(B, SEQ, D_MODEL),
        hs.reshape(DB, S, D_MODEL),
        k_all.reshape(depth, B, tp, N_HEADS, 2, HEAD_DIM),
        v_all,
        jnp.stack(cp),
        jnp.stack(rp),
        jnp.stack(ks).reshape(depth, DB, S, N_HEADS, 2, HEAD_DIM),
        jnp.stack(vs).reshape(depth, DB, S, N_HEADS, V_DIM),
        jnp.stack(cs),
        jnp.stack(rs),
    )
```

```python
import functools
import math

import jax
import jax.numpy as jnp
from jax import lax
from jax.experimental import pallas as pl
from jax.experimental.pallas import tpu as pltpu

F32 = jnp.float32
BF16 = jnp.bfloat16

D_MODEL = 1024
N_HEADS = 8
HEAD_DIM = 64
V_DIM = 128
CHUNK = 64
CHUNK_SHIFT = 6
N_META = 16
CONV_W = 4
LRU_C = 8.0
N_RNN_BLOCKS = 8
RNN_BLOCK = 128
ROPE_THETA = 10000.0
EPS = 1e-6
N_GROUPS = 8
LANES = 128
SUBLANES = 8
G_XR, G_ZR, G_Q, G_K, G_V, G_ZA, G_GR, G_GA = range(8)
Q_SCALE = (HEAD_DIM ** -0.5) * math.log2(math.e)
NEG_BIG = -1e30
ONES_ROWS = 16
KEY_BLOCK = 512


def _sigmoid(x):
    return 0.5 * jnp.tanh(0.5 * x) + 0.5
VMEM_LIMIT = 56 * 1024 * 1024


def _sds(shape, dtype):
    return jax.ShapeDtypeStruct(shape, dtype)


def _nt_dot(a, b):
    return lax.dot_general(a, b, (((1,), (1,)), ((), ())), preferred_element_type=F32)


def _inproj_body(*refs, tm, emit_vt, direct, aliased):
    x_ref, nw_ref, w_ref, cos_ref, sin_ref = refs[:5]
    refs = refs[5 + (2 if aliased else 0):]
    z_ref, refs = refs[0], refs[1:]
    if direct:
        k_ref, v_ref, refs = refs[0], refs[1], refs[2:]
    else:
        kv_ref, refs = refs[0], refs[1:]
    if emit_vt:
        vt_ref, refs = refs[0], refs[1:]
    (xn_ref,) = refs
    j = pl.program_id(1)

    @pl.when(j == 0)
    def _():
        x = x_ref[...]
        ms = jnp.mean(x * x, axis=-1, keepdims=True)
        xn_ref[...] = (x * lax.rsqrt(ms + EPS) * nw_ref[...]).astype(BF16)

    acc = jnp.dot(xn_ref[...], w_ref[...], preferred_element_type=F32)

    def rope_slab(s, first_half):
        rot = jnp.where(first_half, pltpu.roll(s, LANES - 32, 1), pltpu.roll(s, 32, 1))
        return s * cos_ref[...] + rot * sin_ref[...]

    @pl.when(j == G_XR)
    def _():
        z_ref[...] = acc.astype(BF16)

    @pl.when((j == G_ZR) | (j == G_ZA))
    def _():
        z_ref[...] = (acc * _sigmoid(acc)).astype(BF16)

    @pl.when(j == G_Q)
    def _():
        first_half = (lax.broadcasted_iota(jnp.int32, (tm, LANES), 1) & 63) < 32
        for hh in range(N_HEADS):
            sl = slice(hh * LANES, (hh + 1) * LANES)
            z_ref[:, sl] = (rope_slab(acc[:, sl], first_half) * Q_SCALE).astype(BF16)

    @pl.when(j == G_K)
    def _():
        first_half = (lax.broadcasted_iota(jnp.int32, (tm, LANES), 1) & 63) < 32
        for hh in range(N_HEADS):
            sl = slice(hh * LANES, (hh + 1) * LANES)
            r = rope_slab(acc[:, sl], first_half)
            if direct:
                k_ref[0, 0, :, sl] = r
            else:
                kv_ref[:, sl] = r
            z_ref[:, sl] = r.astype(BF16)

    @pl.when(j == G_V)
    def _():
        if direct:
            v_ref[0, 0] = acc.reshape(tm, N_HEADS, V_DIM)
        else:
            kv_ref[...] = acc
        z_ref[...] = acc.astype(BF16)
        if emit_vt:
            for kb in range(tm // KEY_BLOCK):
                vt_ref[kb] = acc[kb * KEY_BLOCK:(kb + 1) * KEY_BLOCK, :].T.astype(BF16)

    @pl.when(j >= G_GR)
    def _():
        z_ref[...] = _sigmoid(acc).astype(BF16)


def _inproj(h2d, nw, w_bf, cos, sin, *, tm, emit_vt=False, direct=None):
    n = h2d.shape[0]
    ntab = cos.shape[0] // tm

    def kv_index(i, j):
        return (i, jnp.where(j > G_K, 1, 0))

    args = [h2d, nw, w_bf, cos, sin]
    in_specs = [
        pl.BlockSpec((tm, D_MODEL), lambda i, j: (i, 0)),
        pl.BlockSpec((1, D_MODEL), lambda i, j: (0, 0)),
        pl.BlockSpec((D_MODEL, D_MODEL), lambda i, j: (0, j)),
        pl.BlockSpec((tm, LANES), lambda i, j: (i % ntab, 0)),
        pl.BlockSpec((tm, LANES), lambda i, j: (i % ntab, 0)),
    ]
    out_specs = [pl.BlockSpec((tm, D_MODEL), lambda i, j: (i, j))]
    out_shape = [_sds((n, N_GROUPS * D_MODEL), BF16)]
    aliases = {}
    if direct is None:
        out_specs.append(pl.BlockSpec((tm, D_MODEL), kv_index))
        out_shape.append(_sds((n, 2 * D_MODEL), F32))
    else:
        layer, depth, bt, t_len, row0, k_buf, v_buf = direct
        nt = t_len // tm
        rows = pl.BlockSpec(
            (pl.Element(1), pl.Element(1), pl.Element(tm), pl.Element(D_MODEL)),
            lambda i, j: (layer, i // nt, pl.multiple_of(row0 + (i % nt) * tm, SUBLANES), 0))
        v_rows = pl.BlockSpec(
            (pl.Element(1), pl.Element(1), pl.Element(tm), pl.Element(N_HEADS), pl.Element(V_DIM)),
            lambda i, j: (layer, i // nt, row0 + (i % nt) * tm, 0, 0))
        out_specs += [rows, v_rows]
        out_shape += [_sds((depth, bt, row0 + t_len, D_MODEL), F32),
                      _sds((depth, bt, row0 + t_len, N_HEADS, V_DIM), F32)]
        if k_buf is not None:
            in_specs += [pl.BlockSpec(memory_space=pl.ANY)] * 2
            args += [k_buf, v_buf]
            aliases = {5: 1, 6: 2}
    if emit_vt:
        out_specs.append(pl.BlockSpec((tm // KEY_BLOCK, D_MODEL, KEY_BLOCK), lambda i, j: (i, 0, 0)))
        out_shape.append(_sds((n // KEY_BLOCK, D_MODEL, KEY_BLOCK), BF16))

    return pl.pallas_call(
        functools.partial(_inproj_body, tm=tm, emit_vt=emit_vt, direct=direct is not None,
                          aliased=bool(aliases)),
        grid=(n // tm, N_GROUPS),
        in_specs=in_specs,
        out_specs=out_specs,
        out_shape=out_shape,
        input_output_aliases=aliases,
        scratch_shapes=[pltpu.VMEM((tm, D_MODEL), BF16)],
        compiler_params=pltpu.CompilerParams(
            dimension_semantics=("arbitrary", "arbitrary"), vmem_limit_bytes=VMEM_LIMIT),
        name="inproj",
    )(*args)


def _meta_rows_body(src_ref, k_in, v_in, k_ref, v_ref):
    del k_in, v_in
    k_ref[0, 0] = src_ref[0, :, :D_MODEL]
    v_ref[0, 0] = src_ref[0, :, D_MODEL:].reshape(N_META, N_HEADS, V_DIM)


def _write_meta_rows(k_buf, v_buf, kvm):
    depth, bt = k_buf.shape[:2]
    rows = pl.BlockSpec((pl.Element(1), pl.Element(1), pl.Element(N_META), pl.Element(D_MODEL)),
                        lambda l, b: (l, b, 0, 0))
    v_rows = pl.BlockSpec((pl.Element(1), pl.Element(1), pl.Element(N_META), pl.Element(N_HEADS),
                           pl.Element(V_DIM)), lambda l, b: (l, b, 0, 0, 0))
    return pl.pallas_call(
        _meta_rows_body,
        grid=(depth, bt),
        in_specs=[pl.BlockSpec((1, N_META, 2 * D_MODEL), lambda l, b: (l, 0, 0)),
                  pl.BlockSpec(memory_space=pl.ANY), pl.BlockSpec(memory_space=pl.ANY)],
        out_specs=[rows, v_rows],
        out_shape=[_sds(k_buf.shape, F32), _sds(v_buf.shape, F32)],
        input_output_aliases={1: 0, 2: 1},
        compiler_params=pltpu.CompilerParams(dimension_semantics=("arbitrary", "arbitrary")),
        name="meta_rows",
    )(kvm, k_buf, v_buf)


def _rglru_body(xr_ref, sz_ref, pre_ref, h0_ref, cw_ref, cb_ref, wg_ref, brg_ref, big_ref, lam_ref,
                o_ref, cst_ref, hst_ref, xbuf, a_s, b_s, h_s, hc, *, tt):
    t = pl.program_id(1)

    @pl.when(t == 0)
    def _():
        xbuf[...] = pre_ref[0]
        hc[...] = jnp.broadcast_to(h0_ref[0], (SUBLANES, D_MODEL))

    cw = cw_ref[...]

    nl = -lam_ref[...]
    softplus = jnp.maximum(nl, 0.0) + jnp.log(1.0 + jnp.exp(-jnp.abs(nl)))
    c = -LRU_C * softplus

    grouped = (tt // SUBLANES, SUBLANES, RNN_BLOCK)
    rowmod = lax.broadcasted_iota(jnp.int32, grouped, 1)
    for n in range(N_RNN_BLOCKS):
        sl = slice(n * RNN_BLOCK, (n + 1) * RNN_BLOCK)
        xg = xr_ref[:, sl].astype(F32).reshape(grouped)
        tail = xbuf[:, sl]
        xc = cb_ref[:, sl] + xg * cw[CONV_W - 1:CONV_W, sl]
        for k in range(1, CONV_W):
            rk = pltpu.roll(xg, k, 1)
            prev = jnp.concatenate([pltpu.roll(tail, k, 0)[None], rk[:-1]], axis=0)
            xc = xc + jnp.where(rowmod >= k, rk, prev) * cw[CONV_W - 1 - k:CONV_W - k, sl]
        xbuf[:, sl] = xg[-1]
        xc = xc.reshape(tt, RNN_BLOCK)
        g = jnp.dot(xc.astype(BF16), wg_ref[n], preferred_element_type=F32)
        r = _sigmoid(g[:, :RNN_BLOCK] + brg_ref[:, sl])
        ig = _sigmoid(g[:, RNN_BLOCK:] + big_ref[:, sl])
        log_a = c[:, sl] * r
        a = jnp.exp(log_a)
        y = -jnp.tanh(log_a) * (1.0 + a * a)
        mult = jnp.where(y > 0.0, y * lax.rsqrt(y), 0.0)
        bb = (mult * ig * xc).reshape(grouped)
        a = a.reshape(grouped)
        for d in (1, 2, 4):
            keep = rowmod >= d
            a_p = jnp.where(keep, pltpu.roll(a, d, 1), 1.0)
            b_p = jnp.where(keep, pltpu.roll(bb, d, 1), 0.0)
            bb = bb + a * b_p
            a = a * a_p
        a_s[:, sl] = a.reshape(tt, RNN_BLOCK)
        b_s[:, sl] = bb.reshape(tt, RNN_BLOCK)

    def group(gi, h):
        rows = pl.ds(pl.multiple_of(gi * SUBLANES, SUBLANES), SUBLANES)
        hb = a_s[rows, :] * h + b_s[rows, :]
        h_s[rows, :] = hb
        return jnp.broadcast_to(hb[SUBLANES - 1:SUBLANES, :], (SUBLANES, D_MODEL))

    h = lax.fori_loop(0, tt // SUBLANES, group, hc[...])
    hc[...] = h
    o_ref[...] = (h_s[...] * sz_ref[...].astype(F32)).astype(BF16)
    hst_ref[0] = h[0:1, :]
    cst_ref[0] = xbuf[...]


def _rglru(z, prefix, h0, cw, cb, wg, brg, big, lam, *, bt, t_len, tt):
    nt = t_len // tt
    n = bt * t_len
    vec = lambda: pl.BlockSpec((1, D_MODEL), lambda b, t: (0, 0))
    return pl.pallas_call(
        functools.partial(_rglru_body, tt=tt),
        grid=(bt, nt),
        in_specs=[
            pl.BlockSpec((tt, D_MODEL), lambda b, t: (b * nt + t, G_XR)),
            pl.BlockSpec((tt, D_MODEL), lambda b, t: (b * nt + t, G_ZR)),
            pl.BlockSpec((1, SUBLANES, D_MODEL), lambda b, t: (b, 0, 0)),
            pl.BlockSpec((1, 1, D_MODEL), lambda b, t: (b, 0, 0)),
            pl.BlockSpec((CONV_W, D_MODEL), lambda b, t: (0, 0)),
            vec(),
            pl.BlockSpec((N_RNN_BLOCKS, RNN_BLOCK, 2 * RNN_BLOCK), lambda b, t: (0, 0, 0)),
            vec(), vec(), vec(),
        ],
        out_specs=[
            pl.BlockSpec((tt, D_MODEL), lambda b, t: (b * nt + t, 0)),
            pl.BlockSpec((1, SUBLANES, D_MODEL), lambda b, t: (b, 0, 0)),
            pl.BlockSpec((1, 1, D_MODEL), lambda b, t: (b, 0, 0)),
        ],
        out_shape=[_sds((n, D_MODEL), BF16), _sds((bt, SUBLANES, D_MODEL), F32),
                   _sds((bt, 1, D_MODEL), F32)],
        scratch_shapes=[
            pltpu.VMEM((SUBLANES, D_MODEL), F32),
            pltpu.VMEM((tt, D_MODEL), F32),
            pltpu.VMEM((tt, D_MODEL), F32),
            pltpu.VMEM((tt, D_MODEL), F32),
            pltpu.VMEM((SUBLANES, D_MODEL), F32),
        ],
        compiler_params=pltpu.CompilerParams(
            dimension_semantics=("arbitrary", "arbitrary"), vmem_limit_bytes=VMEM_LIMIT),
        name="rglru",
    )(z, z, prefix, h0, cw, cb, wg, brg, big, lam)


def _stack_maps(q):
    lane = lax.broadcasted_iota(jnp.int32, q.shape, 1)
    zero = jnp.zeros_like(q)
    return jnp.concatenate(
        [jnp.where(lane < HEAD_DIM, q, zero), jnp.where(lane >= HEAD_DIM, q, zero)], axis=0)


def _softmax_update(pairs, m_s, l_s, acc_s):
    m_prev = m_s[...]
    m_new = m_prev
    for s, _ in pairs:
        m_new = jnp.maximum(m_new, jnp.max(s, axis=-1, keepdims=True))
    alpha = jnp.exp2(m_prev - m_new)
    l_new = alpha * l_s[...]
    acc = alpha * acc_s[...]
    for s, v in pairs:
        p = jnp.exp2(s - m_new)
        l_new = l_new + jnp.sum(p, axis=-1, keepdims=True)
        acc = acc + jnp.dot(p.astype(BF16), v, preferred_element_type=F32)
    m_s[...] = m_new
    l_s[...] = l_new
    acc_s[...] = acc


def _attn_finish(acc, l, sza, subw, lq1, lk1, lq2, lk2, tq, lam_init):
    o = acc * (1.0 / l)
    lam = (jnp.exp(jnp.sum(lq1 * lk1, axis=-1, keepdims=True))
           - jnp.exp(jnp.sum(lq2 * lk2, axis=-1, keepdims=True)) + lam_init)
    od = o[:tq, :] - lam * o[tq:, :]
    ms = jnp.mean(od * od, axis=-1, keepdims=True)
    on = (od * lax.rsqrt(ms + EPS) * subw) * (1.0 - lam_init)
    return (on * sza.astype(F32)).astype(BF16)


def _diag_bias(tq):
    k_chunk = lax.broadcasted_iota(jnp.int32, (tq, 2 * tq), 0) >> CHUNK_SHIFT
    q_chunk = (lax.broadcasted_iota(jnp.int32, (tq, 2 * tq), 1) & (tq - 1)) >> CHUNK_SHIFT
    return jnp.where(k_chunk <= q_chunk, 0.0, -jnp.inf).astype(F32)


def _attn_flat_body(q_ref, k_ref, vt_ref, km_ref, vmt_ref, bias_ref, sza_ref, subc_ref, lq1_ref,
                    lk1_ref, lq2_ref, lk2_ref, one_ref, o_ref, q2t_s, s_buf0, s_buf1, p_buf0, p_buf1, mx_buf0,
                    mx_buf1, al_buf0, al_buf1, pm_s, m_s, acc_s, *, tq, nq, n_extra, lam_init,
                    ticks_per_block):
    s_bufs, p_bufs = (s_buf0, s_buf1), (p_buf0, p_buf1)
    mx_bufs, al_bufs = (mx_buf0, mx_buf1), (al_buf0, al_buf1)
    pairs = [(i, j) for j in range(nq) for i in range(j, nq)]
    n_pairs = len(pairs)

    for i in range(nq):
        qt = q_ref[i * tq:(i + 1) * tq, :].astype(F32).T
        rowq = lax.broadcasted_iota(jnp.int32, qt.shape, 0)
        q2t_s[i] = jnp.concatenate(
            [jnp.where(rowq < HEAD_DIM, qt, 0.0), jnp.where(rowq >= HEAD_DIM, qt, 0.0)],
            axis=1).astype(BF16)
    m_s[...] = jnp.full(m_s.shape, NEG_BIG, F32)
    acc_s[...] = jnp.zeros(acc_s.shape, F32)
    lam = (jnp.exp(jnp.sum(lq1_ref[...] * lk1_ref[...], axis=-1, keepdims=True))
           - jnp.exp(jnp.sum(lq2_ref[...] * lk2_ref[...], axis=-1, keepdims=True)) + lam_init)

    def with_ones(vt):
        return jnp.concatenate([vt, jnp.ones((ONES_ROWS, vt.shape[1]), BF16)], axis=0)

    def stage_a(p):
        i, j = pairs[p]
        st = jnp.dot(k_ref[j * tq:(j + 1) * tq, :], q2t_s[i], preferred_element_type=F32)
        if i == j:
            st = st + bias_ref[...]
        s_bufs[p % 2][...] = st
        mx_bufs[p % 2][...] = jnp.max(st, axis=0, keepdims=True)

    def stage_b(p):
        i, j = pairs[p]
        m_prev = m_s[i]
        m_new = jnp.maximum(m_prev, mx_bufs[p % 2][...])
        if i == j:
            sm = jnp.dot(km_ref[...], q2t_s[i], preferred_element_type=F32)
            rowm = lax.broadcasted_iota(jnp.int32, sm.shape, 0)
            sm = jnp.where(rowm < n_extra, sm, -jnp.inf)
            m_new = jnp.maximum(m_new, jnp.max(sm, axis=0, keepdims=True))
            pm_s[...] = jnp.exp2(sm - m_new).astype(BF16)
        p_bufs[p % 2][...] = jnp.exp2(s_bufs[p % 2][...] - m_new).astype(BF16)
        m_s[i] = m_new
        al_bufs[p % 2][...] = jnp.exp2(m_prev - m_new)

    def stage_c(p):
        i, j = pairs[p]
        u = jnp.dot(with_ones(vt_ref[j]), p_bufs[p % 2][...], preferred_element_type=F32)
        if i == j:
            u = u + jnp.dot(with_ones(vmt_ref[...]), pm_s[...], preferred_element_type=F32)
        acc = al_bufs[p % 2][...] * acc_s[i] + u
        if i != j:
            acc_s[i] = acc
            return
        ot = acc[:V_DIM, :] * (1.0 / acc[V_DIM:V_DIM + 1, :])
        od = ot[:, :tq] - lam * ot[:, tq:]
        ms = jnp.mean(od * od, axis=0, keepdims=True)
        on = (od * lax.rsqrt(ms + EPS) * subc_ref[...]) * (1.0 - lam_init)
        rows = slice(i * tq, (i + 1) * tq)
        o_ref[rows, :] = (on.T * sza_ref[rows, :].astype(F32)).astype(BF16)

    def run_ticks(ticks):
        for t in ticks:
            if 1 <= t <= n_pairs:
                stage_b(t - 1)
            if t >= 2:
                stage_c(t - 2)
            if t < n_pairs:
                stage_a(t)

    one = one_ref[0]
    all_ticks = list(range(n_pairs + 2))
    for start in range(0, len(all_ticks), ticks_per_block):
        pl.when(one > -start)(functools.partial(run_ticks, all_ticks[start:start + ticks_per_block]))


def _attn_flat(z, vt, km, vmt, subc, lq1, lk1, lq2, lk2, *, bt, t_len, n_extra, lam_init,
               ticks_per_block=1):
    tq = vt.shape[2]
    nq = t_len // tq
    n = bt * t_len
    hcol = lambda g: (g * D_MODEL) // LANES
    small = lambda c: pl.BlockSpec((1, c), lambda b, h: (0, 0))
    return pl.pallas_call(
        functools.partial(_attn_flat_body, tq=tq, nq=nq, n_extra=n_extra, lam_init=lam_init,
                          ticks_per_block=ticks_per_block),
        grid=(bt, N_HEADS),
        in_specs=[
            pl.BlockSpec((t_len, LANES), lambda b, h: (b, hcol(G_Q) + h)),
            pl.BlockSpec((t_len, LANES), lambda b, h: (b, hcol(G_K) + h)),
            pl.BlockSpec((nq, LANES, tq), lambda b, h: (b, h, 0)),
            pl.BlockSpec((LANES, LANES), lambda b, h: (0, h)),
            pl.BlockSpec((LANES, LANES), lambda b, h: (h, 0)),
            pl.BlockSpec((tq, 2 * tq), lambda b, h: (0, 0)),
            pl.BlockSpec((t_len, LANES), lambda b, h: (b, hcol(G_ZA) + h)),
            pl.BlockSpec((V_DIM, 1), lambda b, h: (0, 0)),
            small(HEAD_DIM), small(HEAD_DIM), small(HEAD_DIM), small(HEAD_DIM),
            pl.BlockSpec(memory_space=pltpu.SMEM),
        ],
        out_specs=pl.BlockSpec((t_len, LANES), lambda b, h: (b, h)),
        out_shape=_sds((n, D_MODEL), BF16),
        scratch_shapes=[
            pltpu.VMEM((nq, LANES, 2 * tq), BF16),
            pltpu.VMEM((tq, 2 * tq), F32),
            pltpu.VMEM((tq, 2 * tq), F32),
            pltpu.VMEM((tq, 2 * tq), BF16),
            pltpu.VMEM((tq, 2 * tq), BF16),
            pltpu.VMEM((1, 2 * tq), F32),
            pltpu.VMEM((1, 2 * tq), F32),
            pltpu.VMEM((1, 2 * tq), F32),
            pltpu.VMEM((1, 2 * tq), F32),
            pltpu.VMEM((LANES, 2 * tq), BF16),
            pltpu.VMEM((nq, 1, 2 * tq), F32),
            pltpu.VMEM((nq, V_DIM + ONES_ROWS, 2 * tq), F32),
        ],
        compiler_params=pltpu.CompilerParams(
            dimension_semantics=("arbitrary", "arbitrary"), vmem_limit_bytes=VMEM_LIMIT),
        name="attn_flat",
    )(z, z, vt, km, vmt, _diag_bias(tq), z, subc, lq1, lk1, lq2, lk2, jnp.ones((1,), jnp.int32))


def _attn_small_body(*refs, tq, n_own, has_cache, lam_init):
    if has_cache:
        (q_ref, kc_ref, vc_ref, ko_ref, vo_ref, sza_ref, sub_ref, lq1_ref, lk1_ref, lq2_ref,
         lk2_ref, o_ref, m_s, l_s, acc_s) = refs
    else:
        (q_ref, ko_ref, vo_ref, sza_ref, sub_ref, lq1_ref, lk1_ref, lq2_ref,
         lk2_ref, o_ref, m_s, l_s, acc_s) = refs
    q2 = _stack_maps(q_ref[...])
    m_s[...] = jnp.full(m_s.shape, NEG_BIG, F32)
    l_s[...] = jnp.zeros(l_s.shape, F32)
    acc_s[...] = jnp.zeros(acc_s.shape, F32)
    pairs = []
    if has_cache:
        pairs.append((_nt_dot(q2, kc_ref[0, 0].astype(BF16)), vc_ref[0, 0].astype(BF16)))
    so = _nt_dot(q2, ko_ref[...])
    colo = lax.broadcasted_iota(jnp.int32, so.shape, 1)
    so = jnp.where(colo < n_own, so, -jnp.inf)
    pairs.append((so, vo_ref[...]))
    _softmax_update(pairs, m_s, l_s, acc_s)
    o_ref[...] = _attn_finish(acc_s[...], l_s[...], sza_ref[...], sub_ref[...], lq1_ref[...],
                              lk1_ref[...], lq2_ref[...], lk2_ref[...], tq, lam_init)


def _attn_small(z, ko, vo, cache, subw, lq1, lk1, lq2, lk2, *, bt, tq, n_own, layer, lam_init):
    hcol = lambda g: (g * D_MODEL) // LANES
    small = lambda w: pl.BlockSpec((1, w), lambda b, h: (0, 0))
    has_cache = cache is not None
    in_specs = [pl.BlockSpec((tq, LANES), lambda b, h: (b, hcol(G_Q) + h))]
    args = [z]
    if has_cache:
        past = cache[0].shape[2]
        in_specs += [pl.BlockSpec((1, 1, past, LANES), lambda b, h: (layer, b, 0, h))] * 2
        args += list(cache)
    in_specs += [
        pl.BlockSpec((LANES, LANES), lambda b, h: (b, h)),
        pl.BlockSpec((LANES, LANES), lambda b, h: (b, h)),
        pl.BlockSpec((tq, LANES), lambda b, h: (b, hcol(G_ZA) + h)),
        small(V_DIM), small(HEAD_DIM), small(HEAD_DIM), small(HEAD_DIM), small(HEAD_DIM),
    ]
    args += [ko, vo, z, subw, lq1, lk1, lq2, lk2]
    return pl.pallas_call(
        functools.partial(_attn_small_body, tq=tq, n_own=n_own, has_cache=has_cache,
                          lam_init=lam_init),
        grid=(bt, N_HEADS),
        in_specs=in_specs,
        out_specs=pl.BlockSpec((tq, LANES), lambda b, h: (b, h)),
        out_shape=_sds((bt * tq, D_MODEL), BF16),
        scratch_shapes=[
            pltpu.VMEM((2 * tq, 1), F32),
            pltpu.VMEM((2 * tq, 1), F32),
            pltpu.VMEM((2 * tq, V_DIM), F32),
        ],
        compiler_params=pltpu.CompilerParams(
            dimension_semantics=("arbitrary", "arbitrary"), vmem_limit_bytes=VMEM_LIMIT),
        name="attn_small",
    )(*args)


def _merge_body(h_ref, or_ref, oa_ref, gr_ref, ga_ref, wpr_ref, wpa_ref, wo_ref, fnw_ref, out_ref,
                *, final):
    pr = jnp.dot(or_ref[...], wpr_ref[...], preferred_element_type=F32)
    pa = jnp.dot(oa_ref[...], wpa_ref[...], preferred_element_type=F32)
    m = gr_ref[...].astype(F32) * pr + ga_ref[...].astype(F32) * pa
    hn = h_ref[...] + jnp.dot(m.astype(BF16), wo_ref[...], preferred_element_type=F32)
    if final:
        ms = jnp.mean(hn * hn, axis=-1, keepdims=True)
        hn = hn * lax.rsqrt(ms + EPS) * fnw_ref[...]
    out_ref[...] = hn


def _merge(h2d, o_r, o_a, z, wpr, wpa, wo, fnw, *, tm, final):
    n = h2d.shape[0]
    tile = lambda: pl.BlockSpec((tm, D_MODEL), lambda i: (i, 0))
    wspec = lambda: pl.BlockSpec((D_MODEL, D_MODEL), lambda i: (0, 0))
    return pl.pallas_call(
        functools.partial(_merge_body, final=final),
        grid=(n // tm,),
        in_specs=[
            tile(), tile(), tile(),
            pl.BlockSpec((tm, D_MODEL), lambda i: (i, G_GR)),
            pl.BlockSpec((tm, D_MODEL), lambda i: (i, G_GA)),
            wspec(), wspec(), wspec(),
            pl.BlockSpec((1, D_MODEL), lambda i: (0, 0)),
        ],
        out_specs=tile(),
        out_shape=_sds((n, D_MODEL), F32),
        compiler_params=pltpu.CompilerParams(
            dimension_semantics=("arbitrary",), vmem_limit_bytes=VMEM_LIMIT),
        name="merge_out",
    )(h2d, o_r, o_a, z, z, wpr, wpa, wo, fnw)


def _rope_tables(pos):
    half = HEAD_DIM // 2
    inv = 1.0 / (ROPE_THETA ** (jnp.arange(half, dtype=F32) / half))
    ang = pos.astype(F32)[:, None] * inv[None, :]
    cos, sin = jnp.cos(ang), jnp.sin(ang)
    cos_t = jnp.concatenate([cos, cos, cos, cos], axis=1)
    sin_t = jnp.concatenate([-sin, sin, -sin, sin], axis=1)
    return cos_t, sin_t


def _pick_tile(n, pref):
    t = pref
    while n % t:
        t //= 2
    return t


def _pad_rows(x, rows):
    return jnp.pad(x, ((0, rows - x.shape[0]), (0, 0)))


def kernel(x_prompt, x_sample, cache_k, cache_v, state_conv, state_rnn, meta_tokens, norm_w, w_in,
           conv_w, conv_b, w_rg, b_rg, w_ig, b_ig, lru_lambda, lambda_q1, lambda_k1, lambda_q2,
           lambda_k2, subln_w, w_proj_rnn, w_proj_att, w_out, final_norm_w):
    B, SEQ, _ = x_prompt.shape
    DB, S, _ = x_sample.shape
    depth, _, past = cache_k.shape[0], cache_k.shape[1], cache_k.shape[2]
    assert SEQ % 1024 == 0 and S % 16 == 0 and S <= LANES

    w_in_bf = w_in.astype(BF16)
    wg_bf = jnp.concatenate([w_rg, w_ig], axis=-1).astype(BF16)
    wpr_bf, wpa_bf, wo_bf = w_proj_rnn.astype(BF16), w_proj_att.astype(BF16), w_out.astype(BF16)
    cache_k4 = cache_k.reshape(depth, DB, past, D_MODEL)
    cache_v4 = cache_v.reshape(depth, DB, past, D_MODEL)
    fnw = final_norm_w.reshape(1, D_MODEL)

    cos_m, sin_m = _rope_tables(jnp.arange(N_META))
    cos_f, sin_f = _rope_tables(N_META + jnp.arange(SEQ))
    cos_s, sin_s = _rope_tables(N_META + past + jnp.arange(S))
    cos_s, sin_s = jnp.tile(cos_s, (DB, 1)), jnp.tile(sin_s, (DB, 1))

    tm_f = _pick_tile(SEQ, 1024)
    tm_s = _pick_tile(DB * S, 512)
    tt_f = _pick_tile(SEQ, 512)

    def layer_params(l):
        row = lambda a: a[l].reshape(1, -1)
        return dict(
            nw=row(norm_w), w_in=w_in_bf[l], cw=conv_w[l], cb=row(conv_b), wg=wg_bf[l],
            brg=row(b_rg), big=row(b_ig), lam=row(lru_lambda), subw=row(subln_w),
            lq1=row(lambda_q1), lk1=row(lambda_k1), lq2=row(lambda_q2), lk2=row(lambda_k2),
            wpr=wpr_bf[l], wpa=wpa_bf[l], wo=wo_bf[l],
            lam_init=0.8 - 0.6 * math.exp(-0.3 * l))

    def kcols(z):
        return z[:, G_K * D_MODEL:(G_K + 1) * D_MODEL]

    def vcols(z):
        return z[:, G_V * D_MODEL:(G_V + 1) * D_MODEL]

    hm = meta_tokens.astype(F32)
    hf = x_prompt.reshape(B * SEQ, D_MODEL)
    hs = x_sample.reshape(DB * S, D_MODEL)
    zero_prefix = jnp.zeros((1, SUBLANES, D_MODEL), F32)
    zero_h0 = jnp.zeros((1, 1, D_MODEL), F32)

    kp, cp, rp, ks, vs, cs, rs = [], [], [], [], [], [], []
    k_all = v_all = None
    for l in range(depth):
        p = layer_params(l)
        final = l == depth - 1
        lam_kw = dict(subw=p["subw"], lq1=p["lq1"], lk1=p["lk1"], lq2=p["lq2"], lk2=p["lk2"])
        rnn_w = (p["cw"], p["cb"], p["wg"], p["brg"], p["big"], p["lam"])

        zm, kvm = _inproj(hm, p["nw"], p["w_in"], cos_m, sin_m, tm=N_META)
        subc = p["subw"].reshape(V_DIM, 1)
        orm, cst_m, hst_m = _rglru(zm, zero_prefix, zero_h0, *rnn_w, bt=1, t_len=N_META, tt=N_META)
        km_pad, vm_pad = _pad_rows(kcols(zm), LANES), _pad_rows(vcols(zm), LANES)
        oam = _attn_small(zm, km_pad, vm_pad, None, **lam_kw, bt=1, tq=N_META, n_own=N_META,
                          layer=l, lam_init=p["lam_init"])
        hm = _merge(hm, orm, oam, zm, p["wpr"], p["wpa"], p["wo"], fnw, tm=N_META, final=False)

        zf, k_all, v_all, vtf = _inproj(hf, p["nw"], p["w_in"], cos_f, sin_f, tm=tm_f, emit_vt=True,
                                        direct=(l, depth, B, SEQ, N_META, k_all, v_all))
        orf, cst_f, hst_f = _rglru(
            zf, jnp.broadcast_to(cst_m, (B, SUBLANES, D_MODEL)),
            jnp.broadcast_to(hst_m, (B, 1, D_MODEL)), *rnn_w, bt=B, t_len=SEQ, tt=tt_f)
        oaf = _attn_flat(zf, vtf, km_pad, vm_pad.T, subc, p["lq1"], p["lk1"], p["lq2"], p["lk2"],
                         bt=B, t_len=SEQ, n_extra=N_META, lam_init=p["lam_init"])
        hf = _merge(hf, orf, oaf, zf, p["wpr"], p["wpa"], p["wo"], fnw, tm=tt_f, final=final)

        kp.append(kvm)
        cp.append(cst_f[:, SUBLANES - (CONV_W - 1):, :])
        rp.append(hst_f[:, 0, :])

        zs, kvs = _inproj(hs, p["nw"], p["w_in"], cos_s, sin_s, tm=tm_s)
        pre_s = jnp.pad(state_conv[l], ((0, 0), (SUBLANES - (CONV_W - 1), 0), (0, 0)))
        ors, cst_s, hst_s = _rglru(zs, pre_s, state_rnn[l][:, None, :], *rnn_w, bt=DB, t_len=S, tt=S)
        pad_own = lambda a: jnp.pad(a.reshape(DB, S, D_MODEL), ((0, 0), (0, LANES - S), (0, 0))
                                    ).reshape(DB * LANES, D_MODEL)
        oas = _attn_small(zs, pad_own(kcols(zs)), pad_own(vcols(zs)), (cache_k4, cache_v4),
                          **lam_kw, bt=DB, tq=S, n_own=S, layer=l, lam_init=p["lam_init"])
        hs = _merge(hs, ors, oas, zs, p["wpr"], p["wpa"], p["wo"], fnw, tm=tm_s, final=final)

        kvs3 = kvs.reshape(DB, S, 2 * D_MODEL)
        ks.append(kvs3[..., :D_MODEL])
        vs.append(kvs3[..., D_MODEL:])
        cs.append(cst_s[:, SUBLANES - (CONV_W - 1):, :])
        rs.append(hst_s[:, 0, :])

    tp = N_META + SEQ
    k_all, v_all = _write_meta_rows(k_all, v_all, jnp.stack(kp))
    return (
        hf.reshape(B, SEQ, D_MODEL),
        hs.reshape(DB, S, D_MODEL),
        k_all.reshape(depth, B, tp, N_HEADS, 2, HEAD_DIM),
        v_all,
        jnp.stack(cp),
        jnp.stack(rp),
        jnp.stack(ks).reshape(depth, DB, S, N_HEADS, 2, HEAD_DIM),
        jnp.stack(vs).reshape(depth, DB, S, N_HEADS, V_DIM),
        jnp.stack(cs),
        jnp.stack(rs),
    )
```

```python
import functools
import math

import jax
import jax.numpy as jnp
from jax import lax
from jax.experimental import pallas as pl
from jax.experimental.pallas import tpu as pltpu

F32 = jnp.float32
BF16 = jnp.bfloat16

D_MODEL = 1024
N_HEADS = 8
HEAD_DIM = 64
V_DIM = 128
CHUNK = 64
CHUNK_SHIFT = 6
N_META = 16
CONV_W = 4
LRU_C = 8.0
N_RNN_BLOCKS = 8
RNN_BLOCK = 128
ROPE_THETA = 10000.0
EPS = 1e-6
N_GROUPS = 8
LANES = 128
SUBLANES = 8
G_XR, G_ZR, G_Q, G_K, G_V, G_ZA, G_GR, G_GA = range(8)
Q_SCALE = (HEAD_DIM ** -0.5) * math.log2(math.e)
NEG_BIG = -1e30
ONES_ROWS = 16
KEY_BLOCK = 512


def _sigmoid(x):
    return 0.5 * jnp.tanh(0.5 * x) + 0.5
VMEM_LIMIT = 56 * 1024 * 1024


def _sds(shape, dtype):
    return jax.ShapeDtypeStruct(shape, dtype)


def _nt_dot(a, b):
    return lax.dot_general(a, b, (((1,), (1,)), ((), ())), preferred_element_type=F32)


def _inproj_body(*refs, tm, emit_vt, direct, aliased):
    x_ref, nw_ref, w_ref, cos_ref, sin_ref = refs[:5]
    refs = refs[5 + (2 if aliased else 0):]
    z_ref, refs = refs[0], refs[1:]
    if direct:
        k_ref, v_ref, refs = refs[0], refs[1], refs[2:]
    else:
        kv_ref, refs = refs[0], refs[1:]
    if emit_vt:
        vt_ref, refs = refs[0], refs[1:]
    (xn_ref,) = refs
    j = pl.program_id(1)

    @pl.when(j == 0)
    def _():
        x = x_ref[...]
        ms = jnp.mean(x * x, axis=-1, keepdims=True)
        xn_ref[...] = (x * lax.rsqrt(ms + EPS) * nw_ref[...]).astype(BF16)

    acc = jnp.dot(xn_ref[...], w_ref[...], preferred_element_type=F32)

    def rope_slab(s, first_half):
        rot = jnp.where(first_half, pltpu.roll(s, LANES - 32, 1), pltpu.roll(s, 32, 1))
        return s * cos_ref[...] + rot * sin_ref[...]

    @pl.when(j == G_XR)
    def _():
        z_ref[...] = acc.astype(BF16)

    @pl.when((j == G_ZR) | (j == G_ZA))
    def _():
        z_ref[...] = (acc * _sigmoid(acc)).astype(BF16)

    @pl.when(j == G_Q)
    def _():
        first_half = (lax.broadcasted_iota(jnp.int32, (tm, LANES), 1) & 63) < 32
        for hh in range(N_HEADS):
            sl = slice(hh * LANES, (hh + 1) * LANES)
            z_ref[:, sl] = (rope_slab(acc[:, sl], first_half) * Q_SCALE).astype(BF16)

    @pl.when(j == G_K)
    def _():
        first_half = (lax.broadcasted_iota(jnp.int32, (tm, LANES), 1) & 63) < 32
        for hh in range(N_HEADS):
            sl = slice(hh * LANES, (hh + 1) * LANES)
            r = rope_slab(acc[:, sl], first_half)
            if direct:
                k_ref[0, 0, :, sl] = r
            else:
                kv_ref[:, sl] = r
            z_ref[:, sl] = r.astype(BF16)

    @pl.when(j == G_V)
    def _():
        if direct:
            v_ref[0, 0] = acc.reshape(tm, N_HEADS, V_DIM)
        else:
            kv_ref[...] = acc
        z_ref[...] = acc.astype(BF16)
        if emit_vt:
            for kb in range(tm // KEY_BLOCK):
                vt_ref[kb] = acc[kb * KEY_BLOCK:(kb + 1) * KEY_BLOCK, :].T.astype(BF16)

    @pl.when(j >= G_GR)
    def _():
        z_ref[...] = _sigmoid(acc).astype(BF16)


def _inproj(h2d, nw, w_bf, cos, sin, *, tm, emit_vt=False, direct=None):
    n = h2d.shape[0]
    ntab = cos.shape[0] // tm

    def kv_index(i, j):
        return (i, jnp.where(j > G_K, 1, 0))

    args = [h2d, nw, w_bf, cos, sin]
    in_specs = [
        pl.BlockSpec((tm, D_MODEL), lambda i, j: (i, 0)),
        pl.BlockSpec((1, D_MODEL), lambda i, j: (0, 0)),
        pl.BlockSpec((D_MODEL, D_MODEL), lambda i, j: (0, j)),
        pl.BlockSpec((tm, LANES), lambda i, j: (i % ntab, 0)),
        pl.BlockSpec((tm, LANES), lambda i, j: (i % ntab, 0)),
    ]
    out_specs = [pl.BlockSpec((tm, D_MODEL), lambda i, j: (i, j))]
    out_shape = [_sds((n, N_GROUPS * D_MODEL), BF16)]
    aliases = {}
    if direct is None:
        out_specs.append(pl.BlockSpec((tm, D_MODEL), kv_index))
        out_shape.append(_sds((n, 2 * D_MODEL), F32))
    else:
        layer, depth, bt, t_len, row0, k_buf, v_buf = direct
        nt = t_len // tm
        rows = pl.BlockSpec(
            (pl.Element(1), pl.Element(1), pl.Element(tm), pl.Element(D_MODEL)),
            lambda i, j: (layer, i // nt, pl.multiple_of(row0 + (i % nt) * tm, SUBLANES), 0))
        v_rows = pl.BlockSpec(
            (pl.Element(1), pl.Element(1), pl.Element(tm), pl.Element(N_HEADS), pl.Element(V_DIM)),
            lambda i, j: (layer, i // nt, row0 + (i % nt) * tm, 0, 0))
        out_specs += [rows, v_rows]
        out_shape += [_sds((depth, bt, row0 + t_len, D_MODEL), F32),
                      _sds((depth, bt, row0 + t_len, N_HEADS, V_DIM), F32)]
        if k_buf is not None:
            in_specs += [pl.BlockSpec(memory_space=pl.ANY)] * 2
            args += [k_buf, v_buf]
            aliases = {5: 1, 6: 2}
    if emit_vt:
        out_specs.append(pl.BlockSpec((tm // KEY_BLOCK, D_MODEL, KEY_BLOCK), lambda i, j: (i, 0, 0)))
        out_shape.append(_sds((n // KEY_BLOCK, D_MODEL, KEY_BLOCK), BF16))

    return pl.pallas_call(
        functools.partial(_inproj_body, tm=tm, emit_vt=emit_vt, direct=direct is not None,
                          aliased=bool(aliases)),
        grid=(n // tm, N_GROUPS),
        in_specs=in_specs,
        out_specs=out_specs,
        out_shape=out_shape,
        input_output_aliases=aliases,
        scratch_shapes=[pltpu.VMEM((tm, D_MODEL), BF16)],
        compiler_params=pltpu.CompilerParams(
            dimension_semantics=("arbitrary", "arbitrary"), vmem_limit_bytes=VMEM_LIMIT),
        name="inproj",
    )(*args)


def _meta_rows_body(src_ref, k_in, v_in, k_ref, v_ref):
    del k_in, v_in
    k_ref[0, 0] = src_ref[0, :, :D_MODEL]
    v_ref[0, 0] = src_ref[0, :, D_MODEL:].reshape(N_META, N_HEADS, V_DIM)


def _write_meta_rows(k_buf, v_buf, kvm):
    depth, bt = k_buf.shape[:2]
    rows = pl.BlockSpec((pl.Element(1), pl.Element(1), pl.Element(N_META), pl.Element(D_MODEL)),
                        lambda l, b: (l, b, 0, 0))
    v_rows = pl.BlockSpec((pl.Element(1), pl.Element(1), pl.Element(N_META), pl.Element(N_HEADS),
                           pl.Element(V_DIM)), lambda l, b: (l, b, 0, 0, 0))
    return pl.pallas_call(
        _meta_rows_body,
        grid=(depth, bt),
        in_specs=[pl.BlockSpec((1, N_META, 2 * D_MODEL), lambda l, b: (l, 0, 0)),
                  pl.BlockSpec(memory_space=pl.ANY), pl.BlockSpec(memory_space=pl.ANY)],
        out_specs=[rows, v_rows],
        out_shape=[_sds(k_buf.shape, F32), _sds(v_buf.shape, F32)],
        input_output_aliases={1: 0, 2: 1},
        compiler_params=pltpu.CompilerParams(dimension_semantics=("arbitrary", "arbitrary")),
        name="meta_rows",
    )(kvm, k_buf, v_buf)


def _rglru_body(xr_ref, sz_ref, pre_ref, h0_ref, cw_ref, cb_ref, wg_ref, brg_ref, big_ref, lam_ref,
                o_ref, cst_ref, hst_ref, xbuf, a_s, b_s, h_s, hc, *, tt):
    t = pl.program_id(1)

    @pl.when(t == 0)
    def _():
        xbuf[...] = pre_ref[0]
        hc[...] = jnp.broadcast_to(h0_ref[0], (SUBLANES, D_MODEL))

    cw = cw_ref[...]

    nl = -lam_ref[...]
    softplus = jnp.maximum(nl, 0.0) + jnp.log(1.0 + jnp.exp(-jnp.abs(nl)))
    c = -LRU_C * softplus

    grouped = (tt // SUBLANES, SUBLANES, RNN_BLOCK)
    rowmod = lax.broadcasted_iota(jnp.int32, grouped, 1)
    for n in range(N_RNN_BLOCKS):
        sl = slice(n * RNN_BLOCK, (n + 1) * RNN_BLOCK)
        xg = xr_ref[:, sl].astype(F32).reshape(grouped)
        tail = xbuf[:, sl]
        xc = cb_ref[:, sl] + xg * cw[CONV_W - 1:CONV_W, sl]
        for k in range(1, CONV_W):
            rk = pltpu.roll(xg, k, 1)
            prev = jnp.concatenate([pltpu.roll(tail, k, 0)[None], rk[:-1]], axis=0)
            xc = xc + jnp.where(rowmod >= k, rk, prev) * cw[CONV_W - 1 - k:CONV_W - k, sl]
        xbuf[:, sl] = xg[-1]
        xc = xc.reshape(tt, RNN_BLOCK)
        g = jnp.dot(xc.astype(BF16), wg_ref[n], preferred_element_type=F32)
        r = _sigmoid(g[:, :RNN_BLOCK] + brg_ref[:, sl])
        ig = _sigmoid(g[:, RNN_BLOCK:] + big_ref[:, sl])
        log_a = c[:, sl] * r
        a = jnp.exp(log_a)
        y = -jnp.tanh(log_a) * (1.0 + a * a)
        mult = jnp.where(y > 0.0, y * lax.rsqrt(y), 0.0)
        bb = (mult * ig * xc).reshape(grouped)
        a = a.reshape(grouped)
        for d in (1, 2, 4):
            keep = rowmod >= d
            a_p = jnp.where(keep, pltpu.roll(a, d, 1), 1.0)
            b_p = jnp.where(keep, pltpu.roll(bb, d, 1), 0.0)
            bb = bb + a * b_p
            a = a * a_p
        a_s[:, sl] = a.reshape(tt, RNN_BLOCK)
        b_s[:, sl] = bb.reshape(tt, RNN_BLOCK)

    def group(gi, h):
        rows = pl.ds(pl.multiple_of(gi * SUBLANES, SUBLANES), SUBLANES)
        hb = a_s[rows, :] * h + b_s[rows, :]
        h_s[rows, :] = hb
        return jnp.broadcast_to(hb[SUBLANES - 1:SUBLANES, :], (SUBLANES, D_MODEL))

    h = lax.fori_loop(0, tt // SUBLANES, group, hc[...])
    hc[...] = h
    o_ref[...] = (h_s[...] * sz_ref[...].astype(F32)).astype(BF16)
    hst_ref[0] = h[0:1, :]
    cst_ref[0] = xbuf[...]


def _rglru(z, prefix, h0, cw, cb, wg, brg, big, lam, *, bt, t_len, tt):
    nt = t_len // tt
    n = bt * t_len
    vec = lambda: pl.BlockSpec((1, D_MODEL), lambda b, t: (0, 0))
    return pl.pallas_call(
        functools.partial(_rglru_body, tt=tt),
        grid=(bt, nt),
        in_specs=[
            pl.BlockSpec((tt, D_MODEL), lambda b, t: (b * nt + t, G_XR)),
            pl.BlockSpec((tt, D_MODEL), lambda b, t: (b * nt + t, G_ZR)),
            pl.BlockSpec((1, SUBLANES, D_MODEL), lambda b, t: (b, 0, 0)),
            pl.BlockSpec((1, 1, D_MODEL), lambda b, t: (b, 0, 0)),
            pl.BlockSpec((CONV_W, D_MODEL), lambda b, t: (0, 0)),
            vec(),
            pl.BlockSpec((N_RNN_BLOCKS, RNN_BLOCK, 2 * RNN_BLOCK), lambda b, t: (0, 0, 0)),
            vec(), vec(), vec(),
        ],
        out_specs=[
            pl.BlockSpec((tt, D_MODEL), lambda b, t: (b * nt + t, 0)),
            pl.BlockSpec((1, SUBLANES, D_MODEL), lambda b, t: (b, 0, 0)),
            pl.BlockSpec((1, 1, D_MODEL), lambda b, t: (b, 0, 0)),
        ],
        out_shape=[_sds((n, D_MODEL), BF16), _sds((bt, SUBLANES, D_MODEL), F32),
                   _sds((bt, 1, D_MODEL), F32)],
        scratch_shapes=[
            pltpu.VMEM((SUBLANES, D_MODEL), F32),
            pltpu.VMEM((tt, D_MODEL), F32),
            pltpu.VMEM((tt, D_MODEL), F32),
            pltpu.VMEM((tt, D_MODEL), F32),
            pltpu.VMEM((SUBLANES, D_MODEL), F32),
        ],
        compiler_params=pltpu.CompilerParams(
            dimension_semantics=("arbitrary", "arbitrary"), vmem_limit_bytes=VMEM_LIMIT),
        name="rglru",
    )(z, z, prefix, h0, cw, cb, wg, brg, big, lam)


def _stack_maps(q):
    lane = lax.broadcasted_iota(jnp.int32, q.shape, 1)
    zero = jnp.zeros_like(q)
    return jnp.concatenate(
        [jnp.where(lane < HEAD_DIM, q, zero), jnp.where(lane >= HEAD_DIM, q, zero)], axis=0)


def _softmax_update(pairs, m_s, l_s, acc_s):
    m_prev = m_s[...]
    m_new = m_prev
    for s, _ in pairs:
        m_new = jnp.maximum(m_new, jnp.max(s, axis=-1, keepdims=True))
    alpha = jnp.exp2(m_prev - m_new)
    l_new = alpha * l_s[...]
    acc = alpha * acc_s[...]
    for s, v in pairs:
        p = jnp.exp2(s - m_new)
        l_new = l_new + jnp.sum(p, axis=-1, keepdims=True)
        acc = acc + jnp.dot(p.astype(BF16), v, preferred_element_type=F32)
    m_s[...] = m_new
    l_s[...] = l_new
    acc_s[...] = acc


def _attn_finish(acc, l, sza, subw, lq1, lk1, lq2, lk2, tq, lam_init):
    o = acc * (1.0 / l)
    lam = (jnp.exp(jnp.sum(lq1 * lk1, axis=-1, keepdims=True))
           - jnp.exp(jnp.sum(lq2 * lk2, axis=-1, keepdims=True)) + lam_init)
    od = o[:tq, :] - lam * o[tq:, :]
    ms = jnp.mean(od * od, axis=-1, keepdims=True)
    on = (od * lax.rsqrt(ms + EPS) * subw) * (1.0 - lam_init)
    return (on * sza.astype(F32)).astype(BF16)


def _diag_bias(tq):
    k_chunk = lax.broadcasted_iota(jnp.int32, (tq, 2 * tq), 0) >> CHUNK_SHIFT
    q_chunk = (lax.broadcasted_iota(jnp.int32, (tq, 2 * tq), 1) & (tq - 1)) >> CHUNK_SHIFT
    return jnp.where(k_chunk <= q_chunk, 0.0, -jnp.inf).astype(F32)


def _attn_flat_body(q_ref, k_ref, vt_ref, km_ref, vmt_ref, bias_ref, sza_ref, subc_ref, lq1_ref,
                    lk1_ref, lq2_ref, lk2_ref, one_ref, o_ref, q2t_s, s_buf0, s_buf1, p_buf0, p_buf1, mx_buf0,
                    mx_buf1, al_buf0, al_buf1, pm_s, m_s, acc_s, *, tq, nq, n_extra, lam_init,
                    ticks_per_block):
    s_bufs, p_bufs = (s_buf0, s_buf1), (p_buf0, p_buf1)
    mx_bufs, al_bufs = (mx_buf0, mx_buf1), (al_buf0, al_buf1)
    pairs = [(i, j) for j in range(nq) for i in range(j, nq)]
    n_pairs = len(pairs)

    for i in range(nq):
        qt = q_ref[i * tq:(i + 1) * tq, :].astype(F32).T
        rowq = lax.broadcasted_iota(jnp.int32, qt.shape, 0)
        q2t_s[i] = jnp.concatenate(
            [jnp.where(rowq < HEAD_DIM, qt, 0.0), jnp.where(rowq >= HEAD_DIM, qt, 0.0)],
            axis=1).astype(BF16)
    m_s[...] = jnp.full(m_s.shape, NEG_BIG, F32)
    acc_s[...] = jnp.zeros(acc_s.shape, F32)
    lam = (jnp.exp(jnp.sum(lq1_ref[...] * lk1_ref[...], axis=-1, keepdims=True))
           - jnp.exp(jnp.sum(lq2_ref[...] * lk2_ref[...], axis=-1, keepdims=True)) + lam_init)

    def with_ones(vt):
        return jnp.concatenate([vt, jnp.ones((ONES_ROWS, vt.shape[1]), BF16)], axis=0)

    def stage_a(p):
        i, j = pairs[p]
        st = jnp.dot(k_ref[j * tq:(j + 1) * tq, :], q2t_s[i], preferred_element_type=F32)
        if i == j:
            st = st + bias_ref[...]
        s_bufs[p % 2][...] = st
        mx_bufs[p % 2][...] = jnp.max(st, axis=0, keepdims=True)

    def stage_b(p):
        i, j = pairs[p]
        m_prev = m_s[i]
        m_new = jnp.maximum(m_prev, mx_bufs[p % 2][...])
        if i == j:
            sm = jnp.dot(km_ref[...], q2t_s[i], preferred_element_type=F32)
            rowm = lax.broadcasted_iota(jnp.int32, sm.shape, 0)
            sm = jnp.where(rowm < n_extra, sm, -jnp.inf)
            m_new = jnp.maximum(m_new, jnp.max(sm, axis=0, keepdims=True))
            pm_s[...] = jnp.exp2(sm - m_new).astype(BF16)
        p_bufs[p % 2][...] = jnp.exp2(s_bufs[p % 2][...] - m_new).astype(BF16)
        m_s[i] = m_new
        al_bufs[p % 2][...] = jnp.exp2(m_prev - m_new)

    def stage_c(p):
        i, j = pairs[p]
        u = jnp.dot(with_ones(vt_ref[j]), p_bufs[p % 2][...], preferred_element_type=F32)
        if i == j:
            u = u + jnp.dot(with_ones(vmt_ref[...]), pm_s[...], preferred_element_type=F32)
        acc = al_bufs[p % 2][...] * acc_s[i] + u
        if i != j:
            acc_s[i] = acc
            return
        ot = acc[:V_DIM, :] * (1.0 / acc[V_DIM:V_DIM + 1, :])
        od = ot[:, :tq] - lam * ot[:, tq:]
        ms = jnp.mean(od * od, axis=0, keepdims=True)
        on = (od * lax.rsqrt(ms + EPS) * subc_ref[...]) * (1.0 - lam_init)
        rows = slice(i * tq, (i + 1) * tq)
        o_ref[rows, :] = (on.T * sza_ref[rows, :].astype(F32)).astype(BF16)

    def run_ticks(ticks):
        for t in ticks:
            if 1 <= t <= n_pairs:
                stage_b(t - 1)
            if t >= 2:
                stage_c(t - 2)
            if t < n_pairs:
                stage_a(t)

    one = one_ref[0]
    all_ticks = list(range(n_pairs + 2))
    for start in range(0, len(all_ticks), ticks_per_block):
        pl.when(one > -start)(functools.partial(run_ticks, all_ticks[start:start + ticks_per_block]))


def _attn_flat(z, vt, km, vmt, subc, lq1, lk1, lq2, lk2, *, bt, t_len, n_extra, lam_init,
               ticks_per_block=64):
    tq = vt.shape[2]
    nq = t_len // tq
    n = bt * t_len
    hcol = lambda g: (g * D_MODEL) // LANES
    small = lambda c: pl.BlockSpec((1, c), lambda b, h: (0, 0))
    return pl.pallas_call(
        functools.partial(_attn_flat_body, tq=tq, nq=nq, n_extra=n_extra, lam_init=lam_init,
                          ticks_per_block=ticks_per_block),
        grid=(bt, N_HEADS),
        in_specs=[
            pl.BlockSpec((t_len, LANES), lambda b, h: (b, hcol(G_Q) + h)),
            pl.BlockSpec((t_len, LANES), lambda b, h: (b, hcol(G_K) + h)),
            pl.BlockSpec((nq, LANES, tq), lambda b, h: (b, h, 0)),
            pl.BlockSpec((LANES, LANES), lambda b, h: (0, h)),
            pl.BlockSpec((LANES, LANES), lambda b, h: (h, 0)),
            pl.BlockSpec((tq, 2 * tq), lambda b, h: (0, 0)),
            pl.BlockSpec((t_len, LANES), lambda b, h: (b, hcol(G_ZA) + h)),
            pl.BlockSpec((V_DIM, 1), lambda b, h: (0, 0)),
            small(HEAD_DIM), small(HEAD_DIM), small(HEAD_DIM), small(HEAD_DIM),
            pl.BlockSpec(memory_space=pltpu.SMEM),
        ],
        out_specs=pl.BlockSpec((t_len, LANES), lambda b, h: (b, h)),
        out_shape=_sds((n, D_MODEL), BF16),
        scratch_shapes=[
            pltpu.VMEM((nq, LANES, 2 * tq), BF16),
            pltpu.VMEM((tq, 2 * tq), F32),
            pltpu.VMEM((tq, 2 * tq), F32),
            pltpu.VMEM((tq, 2 * tq), BF16),
            pltpu.VMEM((tq, 2 * tq), BF16),
            pltpu.VMEM((1, 2 * tq), F32),
            pltpu.VMEM((1, 2 * tq), F32),
            pltpu.VMEM((1, 2 * tq), F32),
            pltpu.VMEM((1, 2 * tq), F32),
            pltpu.VMEM((LANES, 2 * tq), BF16),
            pltpu.VMEM((nq, 1, 2 * tq), F32),
            pltpu.VMEM((nq, V_DIM + ONES_ROWS, 2 * tq), F32),
        ],
        compiler_params=pltpu.CompilerParams(
            dimension_semantics=("arbitrary", "arbitrary"), vmem_limit_bytes=VMEM_LIMIT),
        name="attn_flat",
    )(z, z, vt, km, vmt, _diag_bias(tq), z, subc, lq1, lk1, lq2, lk2, jnp.ones((1,), jnp.int32))


def _attn_small_body(*refs, tq, n_own, has_cache, lam_init):
    if has_cache:
        (q_ref, kc_ref, vc_ref, ko_ref, vo_ref, sza_ref, sub_ref, lq1_ref, lk1_ref, lq2_ref,
         lk2_ref, o_ref, m_s, l_s, acc_s) = refs
    else:
        (q_ref, ko_ref, vo_ref, sza_ref, sub_ref, lq1_ref, lk1_ref, lq2_ref,
         lk2_ref, o_ref, m_s, l_s, acc_s) = refs
    q2 = _stack_maps(q_ref[...])
    m_s[...] = jnp.full(m_s.shape, NEG_BIG, F32)
    l_s[...] = jnp.zeros(l_s.shape, F32)
    acc_s[...] = jnp.zeros(acc_s.shape, F32)
    pairs = []
    if has_cache:
        pairs.append((_nt_dot(q2, kc_ref[0, 0].astype(BF16)), vc_ref[0, 0].astype(BF16)))
    so = _nt_dot(q2, ko_ref[...])
    colo = lax.broadcasted_iota(jnp.int32, so.shape, 1)
    so = jnp.where(colo < n_own, so, -jnp.inf)
    pairs.append((so, vo_ref[...]))
    _softmax_update(pairs, m_s, l_s, acc_s)
    o_ref[...] = _attn_finish(acc_s[...], l_s[...], sza_ref[...], sub_ref[...], lq1_ref[...],
                              lk1_ref[...], lq2_ref[...], lk2_ref[...], tq, lam_init)


def _attn_small(z, ko, vo, cache, subw, lq1, lk1, lq2, lk2, *, bt, tq, n_own, layer, lam_init):
    hcol = lambda g: (g * D_MODEL) // LANES
    small = lambda w: pl.BlockSpec((1, w), lambda b, h: (0, 0))
    has_cache = cache is not None
    in_specs = [pl.BlockSpec((tq, LANES), lambda b, h: (b, hcol(G_Q) + h))]
    args = [z]
    if has_cache:
        past = cache[0].shape[2]
        in_specs += [pl.BlockSpec((1, 1, past, LANES), lambda b, h: (layer, b, 0, h))] * 2
        args += list(cache)
    in_specs += [
        pl.BlockSpec((LANES, LANES), lambda b, h: (b, h)),
        pl.BlockSpec((LANES, LANES), lambda b, h: (b, h)),
        pl.BlockSpec((tq, LANES), lambda b, h: (b, hcol(G_ZA) + h)),
        small(V_DIM), small(HEAD_DIM), small(HEAD_DIM), small(HEAD_DIM), small(HEAD_DIM),
    ]
    args += [ko, vo, z, subw, lq1, lk1, lq2, lk2]
    return pl.pallas_call(
        functools.partial(_attn_small_body, tq=tq, n_own=n_own, has_cache=has_cache,
                          lam_init=lam_init),
        grid=(bt, N_HEADS),
        in_specs=in_specs,
        out_specs=pl.BlockSpec((tq, LANES), lambda b, h: (b, h)),
        out_shape=_sds((bt * tq, D_MODEL), BF16),
        scratch_shapes=[
            pltpu.VMEM((2 * tq, 1), F32),
            pltpu.VMEM((2 * tq, 1), F32),
            pltpu.VMEM((2 * tq, V_DIM), F32),
        ],
        compiler_params=pltpu.CompilerParams(
            dimension_semantics=("arbitrary", "arbitrary"), vmem_limit_bytes=VMEM_LIMIT),
        name="attn_small",
    )(*args)


def _merge_body(h_ref, or_ref, oa_ref, gr_ref, ga_ref, wpr_ref, wpa_ref, wo_ref, fnw_ref, out_ref,
                *, final):
    pr = jnp.dot(or_ref[...], wpr_ref[...], preferred_element_type=F32)
    pa = jnp.dot(oa_ref[...], wpa_ref[...], preferred_element_type=F32)
    m = gr_ref[...].astype(F32) * pr + ga_ref[...].astype(F32) * pa
    hn = h_ref[...] + jnp.dot(m.astype(BF16), wo_ref[...], preferred_element_type=F32)
    if final:
        ms = jnp.mean(hn * hn, axis=-1, keepdims=True)
        hn = hn * lax.rsqrt(ms + EPS) * fnw_ref[...]
    out_ref[...] = hn


def _merge(h2d, o_r, o_a, z, wpr, wpa, wo, fnw, *, tm, final):
    n = h2d.shape[0]
    tile = lambda: pl.BlockSpec((tm, D_MODEL), lambda i: (i, 0))
    wspec = lambda: pl.BlockSpec((D_MODEL, D_MODEL), lambda i: (0, 0))
    return pl.pallas_call(
        functools.partial(_merge_body, final=final),
        grid=(n // tm,),
        in_specs=[
            tile(), tile(), tile(),
            pl.BlockSpec((tm, D_MODEL), lambda i: (i, G_GR)),
            pl.BlockSpec((tm, D_MODEL), lambda i: (i, G_GA)),
            wspec(), wspec(), wspec(),
            pl.BlockSpec((1, D_MODEL), lambda i: (0, 0)),
        ],
        out_specs=tile(),
        out_shape=_sds((n, D_MODEL), F32),
        compiler_params=pltpu.CompilerParams(
            dimension_semantics=("arbitrary",), vmem_limit_bytes=VMEM_LIMIT),
        name="merge_out",
    )(h2d, o_r, o_a, z, z, wpr, wpa, wo, fnw)


def _rope_tables(pos):
    half = HEAD_DIM // 2
    inv = 1.0 / (ROPE_THETA ** (jnp.arange(half, dtype=F32) / half))
    ang = pos.astype(F32)[:, None] * inv[None, :]
    cos, sin = jnp.cos(ang), jnp.sin(ang)
    cos_t = jnp.concatenate([cos, cos, cos, cos], axis=1)
    sin_t = jnp.concatenate([-sin, sin, -sin, sin], axis=1)
    return cos_t, sin_t


def _pick_tile(n, pref):
    t = pref
    while n % t:
        t //= 2
    return t


def _pad_rows(x, rows):
    return jnp.pad(x, ((0, rows - x.shape[0]), (0, 0)))


def kernel(x_prompt, x_sample, cache_k, cache_v, state_conv, state_rnn, meta_tokens, norm_w, w_in,
           conv_w, conv_b, w_rg, b_rg, w_ig, b_ig, lru_lambda, lambda_q1, lambda_k1, lambda_q2,
           lambda_k2, subln_w, w_proj_rnn, w_proj_att, w_out, final_norm_w):
    B, SEQ, _ = x_prompt.shape
    DB, S, _ = x_sample.shape
    depth, _, past = cache_k.shape[0], cache_k.shape[1], cache_k.shape[2]
    assert SEQ % 1024 == 0 and S % 16 == 0 and S <= LANES

    w_in_bf = w_in.astype(BF16)
    wg_bf = jnp.concatenate([w_rg, w_ig], axis=-1).astype(BF16)
    wpr_bf, wpa_bf, wo_bf = w_proj_rnn.astype(BF16), w_proj_att.astype(BF16), w_out.astype(BF16)
    cache_k4 = cache_k.reshape(depth, DB, past, D_MODEL)
    cache_v4 = cache_v.reshape(depth, DB, past, D_MODEL)
    fnw = final_norm_w.reshape(1, D_MODEL)

    cos_m, sin_m = _rope_tables(jnp.arange(N_META))
    cos_f, sin_f = _rope_tables(N_META + jnp.arange(SEQ))
    cos_s, sin_s = _rope_tables(N_META + past + jnp.arange(S))
    cos_s, sin_s = jnp.tile(cos_s, (DB, 1)), jnp.tile(sin_s, (DB, 1))

    tm_f = _pick_tile(SEQ, 1024)
    tm_s = _pick_tile(DB * S, 512)
    tt_f = _pick_tile(SEQ, 512)

    def layer_params(l):
        row = lambda a: a[l].reshape(1, -1)
        return dict(
            nw=row(norm_w), w_in=w_in_bf[l], cw=conv_w[l], cb=row(conv_b), wg=wg_bf[l],
            brg=row(b_rg), big=row(b_ig), lam=row(lru_lambda), subw=row(subln_w),
            lq1=row(lambda_q1), lk1=row(lambda_k1), lq2=row(lambda_q2), lk2=row(lambda_k2),
            wpr=wpr_bf[l], wpa=wpa_bf[l], wo=wo_bf[l],
            lam_init=0.8 - 0.6 * math.exp(-0.3 * l))

    def kcols(z):
        return z[:, G_K * D_MODEL:(G_K + 1) * D_MODEL]

    def vcols(z):
        return z[:, G_V * D_MODEL:(G_V + 1) * D_MODEL]

    hm = meta_tokens.astype(F32)
    hf = x_prompt.reshape(B * SEQ, D_MODEL)
    hs = x_sample.reshape(DB * S, D_MODEL)
    zero_prefix = jnp.zeros((1, SUBLANES, D_MODEL), F32)
    zero_h0 = jnp.zeros((1, 1, D_MODEL), F32)

    kp, cp, rp, ks, vs, cs, rs = [], [], [], [], [], [], []
    k_all = v_all = None
    for l in range(depth):
        p = layer_params(l)
        final = l == depth - 1
        lam_kw = dict(subw=p["subw"], lq1=p["lq1"], lk1=p["lk1"], lq2=p["lq2"], lk2=p["lk2"])
        rnn_w = (p["cw"], p["cb"], p["wg"], p["brg"], p["big"], p["lam"])

        zm, kvm = _inproj(hm, p["nw"], p["w_in"], cos_m, sin_m, tm=N_META)
        subc = p["subw"].reshape(V_DIM, 1)
        orm, cst_m, hst_m = _rglru(zm, zero_prefix, zero_h0, *rnn_w, bt=1, t_len=N_META, tt=N_META)
        km_pad, vm_pad = _pad_rows(kcols(zm), LANES), _pad_rows(vcols(zm), LANES)
        oam = _attn_small(zm, km_pad, vm_pad, None, **lam_kw, bt=1, tq=N_META, n_own=N_META,
                          layer=l, lam_init=p["lam_init"])
        hm = _merge(hm, orm, oam, zm, p["wpr"], p["wpa"], p["wo"], fnw, tm=N_META, final=False)

        zf, k_all, v_all, vtf = _inproj(hf, p["nw"], p["w_in"], cos_f, sin_f, tm=tm_f, emit_vt=True,
                                        direct=(l, depth, B, SEQ, N_META, k_all, v_all))
        orf, cst_f, hst_f = _rglru(
            zf, jnp.broadcast_to(cst_m, (B, SUBLANES, D_MODEL)),
            jnp.broadcast_to(hst_m, (B, 1, D_MODEL)), *rnn_w, bt=B, t_len=SEQ, tt=tt_f)
        oaf = _attn_flat(zf, vtf, km_pad, vm_pad.T, subc, p["lq1"], p["lk1"], p["lq2"], p["lk2"],
                         bt=B, t_len=SEQ, n_extra=N_META, lam_init=p["lam_init"])
        hf = _merge(hf, orf, oaf, zf, p["wpr"], p["wpa"], p["wo"], fnw, tm=tt_f, final=final)

        kp.append(kvm)
        cp.append(cst_f[:, SUBLANES - (CONV_W - 1):, :])
        rp.append(hst_f[:, 0, :])

        zs, kvs = _inproj(hs, p["nw"], p["w_in"], cos_s, sin_s, tm=tm_s)
        pre_s = jnp.pad(state_conv[l], ((0, 0), (SUBLANES - (CONV_W - 1), 0), (0, 0)))
        ors, cst_s, hst_s = _rglru(zs, pre_s, state_rnn[l][:, None, :], *rnn_w, bt=DB, t_len=S, tt=S)
        pad_own = lambda a: jnp.pad(a.reshape(DB, S, D_MODEL), ((0, 0), (0, LANES - S), (0, 0))
                                    ).reshape(DB * LANES, D_MODEL)
        oas = _attn_small(zs, pad_own(kcols(zs)), pad_own(vcols(zs)), (cache_k4, cache_v4),
                          **lam_kw, bt=DB, tq=S, n_own=S, layer=l, lam_init=p["lam_init"])
        hs = _merge(hs, ors, oas, zs, p["wpr"], p["wpa"], p["wo"], fnw, tm=tm_s, final=final)

        kvs3 = kvs.reshape(DB, S, 2 * D_MODEL)
        ks.append(kvs3[..., :D_MODEL])
        vs.append(kvs3[..., D_MODEL:])
        cs.append(cst_s[:, SUBLANES - (CONV_W - 1):, :])
        rs.append(hst_s[:, 0, :])

    tp = N_META + SEQ
    k_all, v_all = _write_meta_rows(k_all, v_all, jnp.stack(kp))
    return (
        hf.reshape(B, SEQ, D_MODEL),
        hs.reshape(DB, S, D_MODEL),
        k_all.reshape(depth, B, tp, N_HEADS, 2, HEAD_DIM),
        v_all,
        jnp.stack(cp),
        jnp.stack(rp),
        jnp.stack(ks).reshape(depth, DB, S, N_HEADS, 2, HEAD_DIM),
        jnp.stack(vs).reshape(depth, DB, S, N_HEADS, V_DIM),
        jnp.stack(cs),
        jnp.stack(rs),
    )
```
